```python
import functools
import jax, jax.numpy as jnp
from jax import lax
import numpy as np

D_MODEL = 1024
BATCH = 8
SEQ = 2048
DEPTH = 2
DEC_BATCH = 32
DEC_SEQ = 4
PAST_LEN = 8192
PAGE_SIZE = 128

D_MIX = D_MODEL
HEAD_DIM_A = 64
C_A = D_MIX // 4
H_A = C_A // HEAD_DIM_A
HEAD_DIM_B = 64
C_B = D_MIX // 2
H_B = C_B // HEAD_DIM_B
C_CONV = D_MIX - C_A - C_B
CONV_W = 31
W_LORA = 64
A_LORA = 64
G_LORA = 128
RWKV_COLS = 3 * C_A + W_LORA + A_LORA + G_LORA
FOX_COLS = 3 * C_B + H_B
CONV_COLS = 2 * C_CONV
IN_COLS = RWKV_COLS + FOX_COLS + CONV_COLS
Q_BLOCK = 128
N_MEM = 256
H_X = 4
D_X = D_MODEL
HEAD_DIM_X = D_X // H_X
D_FF = 2816
FFN_CONV_W = 3
RMS_EPS = 1e-6
LN_EPS = 1e-5
GN_EPS = 64e-5

kernel_name = 'hymba_rwkv7_fox_conformer_decode_step'


def rms_norm(x, g, eps=RMS_EPS):
    xf = x.astype(jnp.float32)
    y = xf * lax.rsqrt(jnp.mean(xf * xf, axis=-1, keepdims=True) + eps)
    return (y * g.astype(jnp.float32)).astype(x.dtype)


def layer_norm(x, g, b, eps=LN_EPS):
    xf = x.astype(jnp.float32)
    xc = xf - jnp.mean(xf, axis=-1, keepdims=True)
    var = jnp.mean(xc * xc, axis=-1, keepdims=True)
    return (xc * lax.rsqrt(var + eps) * g.astype(jnp.float32) + b.astype(jnp.float32)).astype(x.dtype)


def causal_dwconv(x, buf, w, b):
    xp = jnp.concatenate([buf.astype(x.dtype), x], axis=1)
    y = lax.conv_general_dilated(xp, w[:, None, :].astype(x.dtype), window_strides=(1,), padding='VALID',
                                 dimension_numbers=('NWC', 'WIO', 'NWC'), feature_group_count=x.shape[-1])
    return y + b.astype(x.dtype), xp[:, xp.shape[1] - (w.shape[0] - 1):]


def rwkv7_mix(z, shift_buf, S0, mu, w0, w_up, a0, a_up, g_up, k_k, k_a, r_k, ln_g, ln_b):
    B, T, _ = z.shape
    f32 = jnp.float32
    z_prev = jnp.concatenate([shift_buf.astype(z.dtype), z[:, :-1]], axis=1)
    zs = z + (z_prev - z) * mu.astype(z.dtype)
    r, k, v, wd, ad, gd = jnp.split(
        zs, [C_A, 2 * C_A, 3 * C_A, 3 * C_A + W_LORA, 3 * C_A + W_LORA + A_LORA], axis=-1)
    w_log = -jax.nn.softplus(-(w0 + jnp.tanh(wd) @ w_up).astype(f32)) - 0.5
    decay = jnp.exp(-jnp.exp(w_log))
    a = jax.nn.sigmoid((a0 + ad @ a_up).astype(f32))
    g = jax.nn.sigmoid(gd) @ g_up
    kf = k.astype(f32)
    k_mod = kf * (1.0 + (a - 1.0) * k_a.astype(f32))
    heads = lambda t: t.astype(f32).reshape(B, T, H_A, HEAD_DIM_A)
    kk = heads(kf * k_k.astype(f32))
    kk = kk / jnp.maximum(jnp.sqrt(jnp.sum(kk * kk, axis=-1, keepdims=True)), 1e-12)
    rh, wh, kh, vh, ah = heads(r), heads(decay), heads(k_mod), heads(v), heads(a)

    def step(S, inp):
        r_t, w_t, k_t, v_t, kk_t, a_t = inp
        sa = jnp.einsum('bhvk,bhk->bhv', S, -kk_t)
        S = S * w_t[:, :, None, :] + sa[..., None] * (kk_t * a_t)[:, :, None, :] + v_t[..., None] * k_t[:, :, None, :]
        return S, jnp.einsum('bhvk,bhk->bhv', S, r_t)

    xs = tuple(jnp.moveaxis(t, 1, 0) for t in (rh, wh, kh, vh, kk, ah))
    S_T, ys = lax.scan(step, S0.astype(f32), xs)
    y = jnp.moveaxis(ys, 0, 1)
    yc = y - jnp.mean(y, axis=-1, keepdims=True)
    y = yc * lax.rsqrt(jnp.mean(yc * yc, axis=-1, keepdims=True) + GN_EPS)
    y = y.reshape(B, T, C_A) * ln_g.astype(f32) + ln_b.astype(f32)
    bonus = (jnp.sum(rh * kh * r_k.astype(f32), axis=-1, keepdims=True) * vh).reshape(B, T, C_A)
    out = ((y + bonus) * g.astype(f32)).astype(z.dtype)
    return out, S_T.astype(S0.dtype), z[:, -1:]


def fox_project(zb, q_norm, k_norm, b_f):
    B, T, _ = zb.shape
    q, k, v, fl = jnp.split(zb, [C_B, 2 * C_B, 3 * C_B], axis=-1)
    hd = lambda t: t.reshape(B, T, H_B, HEAD_DIM_B)
    q = rms_norm(hd(q), q_norm)
    k = rms_norm(hd(k), k_norm)
    logf = jax.nn.log_sigmoid((fl + b_f).astype(jnp.float32))
    return q, k, hd(v), logf


def fox_prompt(q, k, v, logf):
    B, T, H, D = q.shape
    scale = D ** -0.5
    cT = jnp.transpose(jnp.cumsum(logf, axis=1), (0, 2, 1))
    kpos = jnp.arange(T)

    def block(i):
        start = i * Q_BLOCK
        qb = lax.dynamic_slice_in_dim(q, start, Q_BLOCK, axis=1)
        cb = lax.dynamic_slice_in_dim(cT, start, Q_BLOCK, axis=2)
        s = jnp.einsum('bqhd,bkhd->bhqk', qb, k, preferred_element_type=jnp.float32) * scale
        s = s + cb[..., :, None] - cT[..., None, :]
        qpos = start + jnp.arange(Q_BLOCK)
        s = jnp.where(kpos[None, :] <= qpos[:, None], s, -jnp.inf)
        p = jax.nn.softmax(s, axis=-1)
        return jnp.einsum('bhqk,bkhd->bqhd', p.astype(v.dtype), v)

    o = lax.map(block, jnp.arange(T // Q_BLOCK))
    return jnp.moveaxis(o, 0, 1).reshape(B, T, H * D)


def fox_sample(q, k, v, logf, kp, vp, lfp):
    B, S, H, D = q.shape
    scale = D ** -0.5
    suffix = lax.cumsum(lfp, axis=1, reverse=True)
    excl = jnp.concatenate([suffix[:, 1:], jnp.zeros_like(suffix[:, :1])], axis=1)
    cT = jnp.transpose(jnp.cumsum(logf, axis=1), (0, 2, 1))
    exT = jnp.transpose(excl, (0, 2, 1))
    s_past = jnp.einsum('bqhd,bkhd->bhqk', q, kp, preferred_element_type=jnp.float32) * scale
    s_past = s_past + cT[..., :, None] + exT[..., None, :]
    s_new = jnp.einsum('bqhd,bkhd->bhqk', q, k, preferred_element_type=jnp.float32) * scale
    s_new = s_new + cT[..., :, None] - cT[..., None, :]
    idx = jnp.arange(S)
    s_new = jnp.where(idx[None, :] <= idx[:, None], s_new, -jnp.inf)
    p = jax.nn.softmax(jnp.concatenate([s_past, s_new], axis=-1), axis=-1)
    P = kp.shape[1]
    o = jnp.einsum('bhqk,bkhd->bqhd', p[..., :P].astype(vp.dtype), vp) + \
        jnp.einsum('bhqk,bkhd->bqhd', p[..., P:].astype(v.dtype), v)
    return o.reshape(B, S, H * D)


def conv_module(zc, buf, conv_w, conv_b, ln_g, ln_b):
    a, b = jnp.split(zc, 2, axis=-1)
    u = a * jax.nn.sigmoid(b)
    y, new_buf = causal_dwconv(u, buf, conv_w, conv_b)
    return jax.nn.silu(layer_norm(y, ln_g, ln_b)), new_buf


def mem_kv(mem, g_mem, w_xk, w_xv, xk_norm):
    B, M, _ = mem.shape
    mn = rms_norm(mem, g_mem)
    k = rms_norm((mn @ w_xk).reshape(B, M, H_X, HEAD_DIM_X), xk_norm)
    v = (mn @ w_xv).reshape(B, M, H_X, HEAD_DIM_X)
    return k, v


def cross_attend(x, mk, mv, g_x, w_xq, xq_norm, w_xo):
    B, T, _ = x.shape
    q = rms_norm((rms_norm(x, g_x) @ w_xq).reshape(B, T, H_X, HEAD_DIM_X), xq_norm)
    s = jnp.einsum('bqhd,bkhd->bhqk', q, mk, preferred_element_type=jnp.float32) * HEAD_DIM_X ** -0.5
    p = jax.nn.softmax(s, axis=-1)
    o = jnp.einsum('bhqk,bkhd->bqhd', p.astype(mv.dtype), mv).reshape(B, T, D_X)
    return o @ w_xo


def conv_ffn(x, buf, g, w_up, cw, cb, w_down):
    h = rms_norm(x, g) @ w_up
    h, new_buf = causal_dwconv(h, buf, cw, cb)
    gate, val = jnp.split(h, 2, axis=-1)
    return (jax.nn.silu(gate) * val) @ w_down, new_buf


def layer_block(x, Wl, S0, shift0, conv0, ffn0, mk, mv, fox_fn):
    z = rms_norm(x, Wl['g_mix']) @ Wl['w_in']
    za, zb, zc = jnp.split(z, [RWKV_COLS, RWKV_COLS + FOX_COLS], axis=-1)
    ya, S, shift = rwkv7_mix(za, shift0, S0, Wl['rwkv_mu'], Wl['rwkv_w0'], Wl['rwkv_w_up'], Wl['rwkv_a0'],
                             Wl['rwkv_a_up'], Wl['rwkv_g_up'], Wl['rwkv_k_k'], Wl['rwkv_k_a'], Wl['rwkv_r_k'],
                             Wl['rwkv_ln_g'], Wl['rwkv_ln_b'])
    q, k, v, logf = fox_project(zb, Wl['fox_q_norm'], Wl['fox_k_norm'], Wl['fox_b_f'])
    yb = fox_fn(q, k, v, logf)
    yc, conv_buf = conv_module(zc, conv0, Wl['conv_w'], Wl['conv_b'], Wl['conv_ln_g'], Wl['conv_ln_b'])
    x = x + jnp.concatenate([ya, yb.astype(x.dtype), yc], axis=-1) @ Wl['w_out']
    x = x + cross_attend(x, mk, mv, Wl['g_x'], Wl['w_xq'], Wl['xq_norm'], Wl['w_xo'])
    f, ffn_buf = conv_ffn(x, ffn0, Wl['g_ffn'], Wl['w_up'], Wl['ffn_conv_w'], Wl['ffn_conv_b'], Wl['w_down'])
    x = x + f
    return x, (k, v, logf, S, shift, conv_buf, ffn_buf)


def setup_inputs(seed: int = 0) -> dict:
    key = jax.random.key(seed)
    keys = jax.random.split(key, 64)
    counter = iter(range(64))
    f32 = jnp.float32

    def nrm(shape, scale=1.0):
        return jax.random.normal(keys[next(counter)], shape, f32) * scale

    L = DEPTH
    n_pages = PAST_LEN // PAGE_SIZE
    n_used = DEC_BATCH * n_pages
    n_phys = n_used + n_used // 4
    perm = jax.random.permutation(keys[next(counter)], n_phys)
    page_table = perm[:n_used].reshape(DEC_BATCH, n_pages).astype(jnp.int32)

    inp = {}
    inp['x_prompt'] = nrm((BATCH, SEQ, D_MODEL))
    inp['x_sample'] = nrm((DEC_BATCH, DEC_SEQ, D_MODEL))
    inp['mem_prompt'] = nrm((BATCH, N_MEM, D_MODEL))
    inp['cache_fox_k'] = nrm((L, n_phys, PAGE_SIZE, H_B, HEAD_DIM_B))
    inp['cache_fox_v'] = nrm((L, n_phys, PAGE_SIZE, H_B, HEAD_DIM_B))
    inp['cache_fox_logf'] = jax.nn.log_sigmoid(3.0 + nrm((L, n_phys, PAGE_SIZE, H_B), 0.5))
    inp['page_table'] = page_table
    inp['state_rwkv'] = nrm((L, DEC_BATCH, H_A, HEAD_DIM_A, HEAD_DIM_A), 0.5)
    inp['state_rwkv_shift'] = nrm((L, DEC_BATCH, 1, RWKV_COLS))
    inp['state_conv'] = nrm((L, DEC_BATCH, CONV_W - 1, C_CONV), 0.5)
    inp['state_ffn'] = nrm((L, DEC_BATCH, FFN_CONV_W - 1, 2 * D_FF))
    inp['cache_mem_k'] = nrm((L, DEC_BATCH, N_MEM, H_X, HEAD_DIM_X))
    inp['cache_mem_v'] = nrm((L, DEC_BATCH, N_MEM, H_X, HEAD_DIM_X))
    inp['g_mix'] = 1.0 + nrm((L, D_MODEL), 0.02)
    inp['w_in'] = nrm((L, D_MODEL, IN_COLS), D_MODEL ** -0.5)
    inp['rwkv_mu'] = jax.random.uniform(keys[next(counter)], (L, RWKV_COLS), f32)
    inp['rwkv_w0'] = nrm((L, C_A), 0.5)
    inp['rwkv_w_up'] = nrm((L, W_LORA, C_A), W_LORA ** -0.5)
    inp['rwkv_a0'] = nrm((L, C_A), 0.1)
    inp['rwkv_a_up'] = nrm((L, A_LORA, C_A), A_LORA ** -0.5)
    inp['rwkv_g_up'] = nrm((L, G_LORA, C_A), G_LORA ** -0.5)
    inp['rwkv_k_k'] = 1.0 + nrm((L, C_A), 0.1)
    inp['rwkv_k_a'] = 1.0 + nrm((L, C_A), 0.1)
    inp['rwkv_r_k'] = nrm((L, H_A, HEAD_DIM_A), 0.1)
    inp['rwkv_ln_g'] = 1.0 + nrm((L, C_A), 0.02)
    inp['rwkv_ln_b'] = nrm((L, C_A), 0.02)
    inp['fox_q_norm'] = 1.0 + nrm((L, HEAD_DIM_B), 0.02)
    inp['fox_k_norm'] = 1.0 + nrm((L, HEAD_DIM_B), 0.02)
    inp['fox_b_f'] = 3.0 + nrm((L, H_B), 0.5)
    inp['conv_w'] = nrm((L, CONV_W, C_CONV), CONV_W ** -0.5)
    inp['conv_b'] = nrm((L, C_CONV), 0.02)
    inp['conv_ln_g'] = 1.0 + nrm((L, C_CONV), 0.02)
    inp['conv_ln_b'] = nrm((L, C_CONV), 0.02)
    inp['w_out'] = nrm((L, D_MIX, D_MODEL), D_MIX ** -0.5)
    inp['g_x'] = 1.0 + nrm((L, D_MODEL), 0.02)
    inp['g_mem'] = 1.0 + nrm((L, D_MODEL), 0.02)
    inp['w_xq'] = nrm((L, D_MODEL, D_X), D_MODEL ** -0.5)
    inp['w_xk'] = nrm((L, D_MODEL, D_X), D_MODEL ** -0.5)
    inp['w_xv'] = nrm((L, D_MODEL, D_X), D_MODEL ** -0.5)
    inp['xq_norm'] = 1.0 + nrm((L, HEAD_DIM_X), 0.02)
    inp['xk_norm'] = 1.0 + nrm((L, HEAD_DIM_X), 0.02)
    inp['w_xo'] = nrm((L, D_X, D_MODEL), D_X ** -0.5)
    inp['g_ffn'] = 1.0 + nrm((L, D_MODEL), 0.02)
    inp['w_up'] = nrm((L, D_MODEL, 2 * D_FF), D_MODEL ** -0.5)
    inp['ffn_conv_w'] = nrm((L, FFN_CONV_W, 2 * D_FF), FFN_CONV_W ** -0.5)
    inp['ffn_conv_b'] = nrm((L, 2 * D_FF), 0.02)
    inp['w_down'] = nrm((L, D_FF, D_MODEL), D_FF ** -0.5)
    return inp


def reference(x_prompt, x_sample, mem_prompt, cache_fox_k, cache_fox_v, cache_fox_logf, page_table,
              state_rwkv, state_rwkv_shift, state_conv, state_ffn, cache_mem_k, cache_mem_v,
              g_mix, w_in, rwkv_mu, rwkv_w0, rwkv_w_up, rwkv_a0, rwkv_a_up, rwkv_g_up, rwkv_k_k, rwkv_k_a,
              rwkv_r_k, rwkv_ln_g, rwkv_ln_b, fox_q_norm, fox_k_norm, fox_b_f, conv_w, conv_b, conv_ln_g,
              conv_ln_b, w_out, g_x, g_mem, w_xq, w_xk, w_xv, xq_norm, xk_norm, w_xo, g_ffn, w_up,
              ffn_conv_w, ffn_conv_b, w_down):
    W = dict(g_mix=g_mix, w_in=w_in, rwkv_mu=rwkv_mu, rwkv_w0=rwkv_w0, rwkv_w_up=rwkv_w_up, rwkv_a0=rwkv_a0,
             rwkv_a_up=rwkv_a_up, rwkv_g_up=rwkv_g_up, rwkv_k_k=rwkv_k_k, rwkv_k_a=rwkv_k_a, rwkv_r_k=rwkv_r_k,
             rwkv_ln_g=rwkv_ln_g, rwkv_ln_b=rwkv_ln_b, fox_q_norm=fox_q_norm, fox_k_norm=fox_k_norm,
             fox_b_f=fox_b_f, conv_w=conv_w, conv_b=conv_b, conv_ln_g=conv_ln_g, conv_ln_b=conv_ln_b,
             w_out=w_out, g_x=g_x, g_mem=g_mem, w_xq=w_xq, w_xk=w_xk, w_xv=w_xv, xq_norm=xq_norm,
             xk_norm=xk_norm, w_xo=w_xo, g_ffn=g_ffn, w_up=w_up, ffn_conv_w=ffn_conv_w,
             ffn_conv_b=ffn_conv_b, w_down=w_down)

    B = x_prompt.shape[0]
    dt = x_prompt.dtype
    x = x_prompt
    p_st, p_mk, p_mv = [], [], []
    for l in range(DEPTH):
        Wl = {n: a[l] for n, a in W.items()}
        mk, mv = mem_kv(mem_prompt, Wl['g_mem'], Wl['w_xk'], Wl['w_xv'], Wl['xk_norm'])
        S0 = jnp.zeros((B, H_A, HEAD_DIM_A, HEAD_DIM_A), dt)
        shift0 = jnp.zeros((B, 1, RWKV_COLS), dt)
        conv0 = jnp.zeros((B, CONV_W - 1, C_CONV), dt)
        ffn0 = jnp.zeros((B, FFN_CONV_W - 1, 2 * D_FF), dt)
        x, st = layer_block(x, Wl, S0, shift0, conv0, ffn0, mk, mv, fox_prompt)
        p_st.append(st)
        p_mk.append(mk)
        p_mv.append(mv)
    y_prompt = x
    p_fox_k, p_fox_v, p_fox_logf, p_rwkv, p_rwkv_shift, p_conv, p_ffn = (jnp.stack(f) for f in zip(*p_st))
    p_mem_k = jnp.stack(p_mk)
    p_mem_v = jnp.stack(p_mv)

    DB = x_sample.shape[0]
    x = x_sample
    s_st = []
    for l in range(DEPTH):
        Wl = {n: a[l] for n, a in W.items()}
        kp = cache_fox_k[l][page_table].reshape(DB, -1, H_B, HEAD_DIM_B)
        vp = cache_fox_v[l][page_table].reshape(DB, -1, H_B, HEAD_DIM_B)
        lfp = cache_fox_logf[l][page_table].reshape(DB, -1, H_B)
        fox_fn = functools.partial(fox_sample, kp=kp, vp=vp, lfp=lfp)
        x, st = layer_block(x, Wl, state_rwkv[l], state_rwkv_shift[l], state_conv[l], state_ffn[l],
                            cache_mem_k[l], cache_mem_v[l], fox_fn)
        s_st.append(st)
    y_sample = x
    s_fox_k, s_fox_v, s_fox_logf, s_rwkv, s_rwkv_shift, s_conv, s_ffn = (jnp.stack(f) for f in zip(*s_st))

    return (y_prompt, y_sample, p_fox_k, p_fox_v, p_fox_logf, p_rwkv, p_rwkv_shift, p_conv, p_ffn,
            p_mem_k, p_mem_v, s_fox_k, s_fox_v, s_fox_logf, s_rwkv, s_rwkv_shift, s_conv, s_ffn)
```

```python
import functools

import jax
import jax.numpy as jnp
from jax import lax
from jax.experimental import pallas as pl
from jax.experimental.pallas import tpu as pltpu

F32 = jnp.float32
BF16 = jnp.bfloat16

D_MODEL = 1024
HEAD_DIM_A = 64
C_A = 256
H_A = 4
W_LORA = 64
A_LORA = 64
G_LORA = 128
RWKV_COLS = 3 * C_A + W_LORA + A_LORA + G_LORA
HEAD_DIM_B = 64
C_B = 512
H_B = 8
C_CONV = 256
CONV_W = 31
N_MEM = 256
H_X = 4
HEAD_DIM_X = 256
D_FF = 2816
FFN_CONV_W = 3
PAGE_SIZE = 128
RMS_EPS = 1e-6
LN_EPS = 1e-5
GN_EPS = 64e-5

COL_Q = RWKV_COLS
COL_K = COL_Q + C_B
COL_V = COL_K + C_B
COL_C = COL_V + C_B
COL_F = COL_C + 2 * C_CONV
LANE = 128
IN_COLS_PAD = COL_F + LANE

VMEM_LIMIT_BYTES = 56 * 1024 * 1024
NEG_BIG = -1e30


def _cparams(*sem):
    return pltpu.CompilerParams(dimension_semantics=sem, vmem_limit_bytes=VMEM_LIMIT_BYTES)


def _dot(a, b):
    return jnp.dot(a.astype(BF16), b.astype(BF16), preferred_element_type=F32)


def _dot_nt(a, b):
    return lax.dot_general(a.astype(BF16), b.astype(BF16), (((1,), (1,)), ((), ())),
                           preferred_element_type=F32)


def _dot_tn(a, b):
    return lax.dot_general(a.astype(BF16), b.astype(BF16), (((0,), (0,)), ((), ())),
                           preferred_element_type=F32)


def _split_bf16(x, n):
    parts, r = [], x
    for i in range(n):
        p = r.astype(BF16)
        parts.append(p)
        if i + 1 < n:
            r = r - p.astype(F32)
    return parts


def _dot_split_lhs(x, m_bf, n=3):
    acc = None
    for p in _split_bf16(x, n):
        d = jnp.dot(p, m_bf, preferred_element_type=F32)
        acc = d if acc is None else acc + d
    return acc


def _dot_split_rhs(m_bf, x, n=3):
    acc = None
    for p in _split_bf16(x, n):
        d = jnp.dot(m_bf, p, preferred_element_type=F32)
        acc = d if acc is None else acc + d
    return acc


def _dot_nt_split_rhs(m_bf, x, n=3):
    acc = None
    for p in _split_bf16(x, n):
        d = lax.dot_general(m_bf, p, (((1,), (1,)), ((), ())), preferred_element_type=F32)
        acc = d if acc is None else acc + d
    return acc


def _iota(shape, axis):
    return lax.broadcasted_iota(jnp.int32, shape, axis)


def _block_ones(n, seg):
    return (_iota((n, n), 0) // seg == _iota((n, n), 1) // seg).astype(BF16)


def _sigmoid(x):
    return 1.0 / (1.0 + jnp.exp(-x))


def _softplus(x):
    return jnp.maximum(x, 0.0) + jnp.log(1.0 + jnp.exp(-jnp.abs(x)))


def _rms(x, g):
    ms = jnp.mean(x * x, axis=-1, keepdims=True)
    return x * lax.rsqrt(ms + RMS_EPS) * g


def _norm_matmul_kernel(x_ref, g_ref, w_ref, o_ref):
    xn = _rms(x_ref[...], g_ref[...])
    o_ref[...] = jnp.dot(xn.astype(BF16), w_ref[...], preferred_element_type=F32)


def _norm_matmul(x2d, g, w_bf, tm):
    m, d = x2d.shape
    n = w_bf.shape[1]
    return pl.pallas_call(
        _norm_matmul_kernel,
        grid=(m // tm,),
        in_specs=[pl.BlockSpec((tm, d), lambda i: (i, 0)),
                  pl.BlockSpec((1, d), lambda i: (0, 0)),
                  pl.BlockSpec((d, n), lambda i: (0, 0))],
        out_specs=pl.BlockSpec((tm, n), lambda i: (i, 0)),
        out_shape=jax.ShapeDtypeStruct((m, n), F32),
        compiler_params=_cparams("parallel"),
        name="norm_matmul",
    )(x2d, g.reshape(1, d), w_bf)


def _mem_kv_kernel(m_ref, g_ref, wk_ref, wv_ref, kn_ref, k_out, v_out):
    xn = _rms(m_ref[...], g_ref[...]).astype(BF16)
    kraw = jnp.dot(xn, wk_ref[...], preferred_element_type=F32)
    v_out[...] = jnp.dot(xn, wv_ref[...], preferred_element_type=F32)
    for h in range(H_X):
        sl = slice(h * HEAD_DIM_X, (h + 1) * HEAD_DIM_X)
        k_out[:, sl] = _rms(kraw[:, sl], kn_ref[...])


def _mem_kv(mem2d, g, wk_bf, wv_bf, k_norm, tm):
    m, d = mem2d.shape
    row = lambda i: (i, 0)
    fix = lambda i: (0, 0)
    return pl.pallas_call(
        _mem_kv_kernel,
        grid=(m // tm,),
        in_specs=[pl.BlockSpec((tm, d), row), pl.BlockSpec((1, d), fix),
                  pl.BlockSpec((d, d), fix), pl.BlockSpec((d, d), fix),
                  pl.BlockSpec((1, HEAD_DIM_X), fix)],
        out_specs=[pl.BlockSpec((tm, d), row), pl.BlockSpec((tm, d), row)],
        out_shape=[jax.ShapeDtypeStruct((m, d), F32)] * 2,
        compiler_params=_cparams("parallel"),
        name="mem_kv",
    )(mem2d, g.reshape(1, d), wk_bf, wv_bf, k_norm.reshape(1, HEAD_DIM_X))


def _rwkv_kernel(z_ref, shift_ref, s0_ref, mu_ref, w0_ref, wup_ref, a0_ref, aup_ref, gup_ref,
                 kk_ref, ka_ref, rk_ref, lng_ref, lnb_ref,
                 y_ref, st_ref,
                 s_scr, zprev_scr, lw_scr, kn_scr, be_scr, km_scr, r_scr, v_scr, yacc_scr,
                 *, tb, chunk, t_valid):
    ti = pl.program_id(1)

    @pl.when(ti == 0)
    def _():
        s_scr[...] = s0_ref[0]
        zprev_scr[...] = shift_ref[0]

    z = z_ref[0]
    row = _iota((tb, 1), 0)
    zprev = jnp.where(row == 0, zprev_scr[...], pltpu.roll(z, 1, axis=0))
    zprev_scr[...] = z[tb - 1:tb, :]
    zs = z + (zprev - z) * mu_ref[...]
    r = zs[:, 0:C_A]
    k = zs[:, C_A:2 * C_A]
    v = zs[:, 2 * C_A:3 * C_A]
    o_w = 3 * C_A
    wd = zs[:, o_w:o_w + W_LORA]
    ad = zs[:, o_w + W_LORA:o_w + W_LORA + A_LORA]
    gd = zs[:, o_w + W_LORA + A_LORA:RWKV_COLS]
    w_log = -_softplus(-(w0_ref[...] + _dot(jnp.tanh(wd), wup_ref[...]))) - 0.5
    lw = -jnp.exp(w_log)
    a = _sigmoid(a0_ref[...] + _dot(ad, aup_ref[...]))
    g = _dot(_sigmoid(gd), gup_ref[...])
    k_mod = k * (1.0 + (a - 1.0) * ka_ref[...])
    bd = _block_ones(C_A, HEAD_DIM_A)
    kk0 = k * kk_ref[...]
    kn = kk0 / jnp.maximum(jnp.sqrt(_dot_split_lhs(kk0 * kk0, bd)), 1e-12)
    if t_valid < tb:
        ok = row < t_valid
        lw = jnp.where(ok, lw, 0.0)
        kn = jnp.where(ok, kn, 0.0)
        k_mod = jnp.where(ok, k_mod, 0.0)
    lw_scr[...] = lw
    kn_scr[...] = kn
    be_scr[...] = kn * a
    km_scr[...] = k_mod
    r_scr[...] = r
    v_scr[...] = v

    c = chunk
    ri = _iota((c, c), 0)
    ci = _iota((c, c), 1)
    lincl = (ri >= ci).astype(BF16)
    m_strict = ri > ci
    m_incl = ri >= ci
    eye = (ri == ci).astype(F32)

    def chunk_body(j, carry):
        rows = pl.ds(pl.multiple_of(j * c, c), c)
        lw_c = lw_scr[rows, :]
        cum = _dot_split_rhs(lincl, lw_c)
        g_inc = jnp.exp(cum)
        g_exc = jnp.exp(cum - lw_c)
        g_inv = jnp.exp(-cum)
        g_end = g_inc[c - 1:c, :]
        at = -kn_scr[rows, :] * g_exc
        bt = be_scr[rows, :] * g_inv
        kt = km_scr[rows, :] * g_inv
        rt = r_scr[rows, :] * g_inc
        vv = v_scr[rows, :]
        btc = bt * g_end
        ktc = kt * g_end
        ys = []
        for h in range(H_A):
            sl = slice(h * HEAD_DIM_A, (h + 1) * HEAD_DIM_A)
            s_h = s_scr[h]
            at_h, bt_h, kt_h, rt_h, v_h = at[:, sl], bt[:, sl], kt[:, sl], rt[:, sl], vv[:, sl]
            a_ab = jnp.where(m_strict, _dot_nt(at_h, bt_h), 0.0)
            a_ak = jnp.where(m_strict, _dot_nt(at_h, kt_h), 0.0)
            x = eye + a_ab
            p = a_ab
            n = 1
            while 2 * n < c:
                p = _dot(p, p)
                x = x + _dot(p, x)
                n *= 2
            u = _dot(x, _dot_nt(at_h, s_h) + _dot(a_ak, v_h))
            y_h = (_dot_nt(rt_h, s_h)
                   + _dot(jnp.where(m_incl, _dot_nt(rt_h, bt_h), 0.0), u)
                   + _dot(jnp.where(m_incl, _dot_nt(rt_h, kt_h), 0.0), v_h))
            s_scr[h] = s_h * g_end[:, sl] + _dot_tn(u, btc[:, sl]) + _dot_tn(v_h, ktc[:, sl])
            ys.append(y_h)
        yacc_scr[rows, :] = jnp.concatenate(ys, axis=1)
        return carry

    lax.fori_loop(0, tb // c, chunk_body, 0)

    y = yacc_scr[...]
    inv_n = 1.0 / HEAD_DIM_A
    mean = _dot_split_lhs(y, bd) * inv_n
    yc = y - mean
    var = _dot_split_lhs(yc * yc, bd) * inv_n
    yn = yc * lax.rsqrt(var + GN_EPS) * lng_ref[...] + lnb_ref[...]
    bonus = _dot_split_lhs(r * k_mod * rk_ref[...], bd) * v
    y_ref[0] = ((yn + bonus) * g).astype(y_ref.dtype)

    @pl.when(ti == pl.num_programs(1) - 1)
    def _():
        st_ref[0] = s_scr[...]


def _rwkv(z3, shift0, s0, wl, *, tb, chunk, t_valid, out_dtype=BF16):
    b, t, _ = z3.shape
    vec = lambda n: pl.BlockSpec((1, n), lambda i, j: (0, 0))
    mat = lambda r, c_: pl.BlockSpec((r, c_), lambda i, j: (0, 0))
    kern = functools.partial(_rwkv_kernel, tb=tb, chunk=chunk, t_valid=t_valid)
    scr = [pltpu.VMEM((H_A, HEAD_DIM_A, HEAD_DIM_A), F32), pltpu.VMEM((1, RWKV_COLS), F32)]
    scr += [pltpu.VMEM((tb, C_A), F32)] * 7
    return pl.pallas_call(
        kern,
        grid=(b, t // tb),
        in_specs=[pl.BlockSpec((1, tb, RWKV_COLS), lambda i, j: (i, j, 0)),
                  pl.BlockSpec((1, 1, RWKV_COLS), lambda i, j: (i, 0, 0)),
                  pl.BlockSpec((1, H_A, HEAD_DIM_A, HEAD_DIM_A), lambda i, j: (i, 0, 0, 0)),
                  vec(RWKV_COLS), vec(C_A), mat(W_LORA, C_A), vec(C_A), mat(A_LORA, C_A),
                  mat(G_LORA, C_A), vec(C_A), vec(C_A), vec(C_A), vec(C_A), vec(C_A)],
        out_specs=[pl.BlockSpec((1, tb, C_A), lambda i, j: (i, j, 0)),
                   pl.BlockSpec((1, H_A, HEAD_DIM_A, HEAD_DIM_A), lambda i, j: (i, 0, 0, 0))],
        out_shape=[jax.ShapeDtypeStruct((b, t, C_A), out_dtype),
                   jax.ShapeDtypeStruct((b, H_A, HEAD_DIM_A, HEAD_DIM_A), F32)],
        scratch_shapes=scr,
        compiler_params=_cparams("parallel", "arbitrary"),
        name="rwkv7",
    )(z3, shift0, s0, wl['rwkv_mu'].reshape(1, -1), wl['rwkv_w0'].reshape(1, -1), wl['rwkv_w_up_bf'],
      wl['rwkv_a0'].reshape(1, -1), wl['rwkv_a_up_bf'], wl['rwkv_g_up_bf'],
      wl['rwkv_k_k'].reshape(1, -1), wl['rwkv_k_a'].reshape(1, -1), wl['rwkv_r_k'].reshape(1, -1),
      wl['rwkv_ln_g'].reshape(1, -1), wl['rwkv_ln_b'].reshape(1, -1))


def _fox_norms(zq, zk, fl, qn_g, kn_g, bf):
    bd = _block_ones(C_B, HEAD_DIM_B)
    inv_n = 1.0 / HEAD_DIM_B
    q = zq * lax.rsqrt(_dot_split_lhs(zq * zq, bd, 2) * inv_n + RMS_EPS) * qn_g
    k = zk * lax.rsqrt(_dot_split_lhs(zk * zk, bd, 2) * inv_n + RMS_EPS) * kn_g
    logf = -_softplus(-(fl + bf))
    return q * (HEAD_DIM_B ** -0.5), k, logf


def _fox_norm_kernel(zq_ref, zk_ref, fl_ref, qn_ref, kn_ref, bf_ref, q_out, k_out, lf_out):
    q, k, logf = _fox_norms(zq_ref[...], zk_ref[...], fl_ref[...], qn_ref[...], kn_ref[...], bf_ref[...])
    q_out[...] = q
    k_out[...] = k
    lf_out[...] = logf


def _fox_norm(z2, qn_t, kn_t, bf_pad):
    m = z2.shape[0]
    fix = lambda i: (0, 0)
    return pl.pallas_call(
        _fox_norm_kernel,
        grid=(1,),
        in_specs=[pl.BlockSpec((m, C_B), lambda i: (0, COL_Q // C_B)),
                  pl.BlockSpec((m, C_B), lambda i: (0, COL_K // C_B)),
                  pl.BlockSpec((m, LANE), lambda i: (0, COL_F // LANE)),
                  pl.BlockSpec((1, C_B), fix), pl.BlockSpec((1, C_B), fix), pl.BlockSpec((1, LANE), fix)],
        out_specs=[pl.BlockSpec((m, C_B), fix), pl.BlockSpec((m, C_B), fix), pl.BlockSpec((m, LANE), fix)],
        out_shape=[jax.ShapeDtypeStruct((m, C_B), F32), jax.ShapeDtypeStruct((m, C_B), F32),
                   jax.ShapeDtypeStruct((m, LANE), F32)],
        compiler_params=_cparams("arbitrary"),
        name="fox_norm",
    )(z2, z2, z2, qn_t, kn_t, bf_pad)


N_AUG = 3


def _fox_prep_kernel(zq_ref, zk_ref, zv_ref, fl_ref, qn_ref, kn_ref, bf_ref,
                     k_out, lf_out, qa_out, ka_out, vb_out, cum_scr, *, tp):
    ti = pl.program_id(1)

    @pl.when(ti == 0)
    def _():
        cum_scr[...] = jnp.zeros_like(cum_scr)

    q, k, logf = _fox_norms(zq_ref[0], zk_ref[0], fl_ref[0], qn_ref[...], kn_ref[...], bf_ref[...])
    k_out[0] = k
    lf_out[0] = logf
    vb_out[0] = zv_ref[0].astype(BF16)
    lincl = (_iota((tp, tp), 0) >= _iota((tp, tp), 1)).astype(BF16)
    cum = _dot_split_rhs(lincl, logf) + cum_scr[...]
    cum_scr[...] = cum[tp - 1:tp, :]
    lane = _iota((tp, LANE), 1)
    for h in range(H_B):
        c_h = jnp.broadcast_to(cum[:, h:h + 1], (tp, LANE))
        pieces = [p.astype(F32) for p in _split_bf16(c_h, N_AUG)]
        aug_q = jnp.zeros((tp, LANE), F32)
        aug_k = jnp.zeros((tp, LANE), F32)
        for i, p in enumerate(pieces):
            aug_q = jnp.where(lane == HEAD_DIM_B + i, p, aug_q)
            aug_k = jnp.where(lane == HEAD_DIM_B + N_AUG + i, -p, aug_k)
        aug_q = jnp.where((lane >= HEAD_DIM_B + N_AUG) & (lane < HEAD_DIM_B + 2 * N_AUG), 1.0, aug_q)
        aug_k = jnp.where((lane >= HEAD_DIM_B) & (lane < HEAD_DIM_B + N_AUG), 1.0, aug_k)
        blk = slice((h // 2) * LANE, (h // 2 + 1) * LANE)
        q2, k2 = q[:, blk], k[:, blk]
        if h % 2 == 1:
            q2 = pltpu.roll(q2, HEAD_DIM_B, axis=1)
            k2 = pltpu.roll(k2, HEAD_DIM_B, axis=1)
        qa_out[0, h] = jnp.where(lane < HEAD_DIM_B, q2, aug_q).astype(BF16)
        ka_out[0, h] = jnp.where(lane < HEAD_DIM_B, k2, aug_k).astype(BF16)


def _fox_prep(z3, qn_t, kn_t, bf_pad, tp):
    b, t, _ = z3.shape
    fix = lambda i, j: (0, 0)
    kern = functools.partial(_fox_prep_kernel, tp=tp)
    return pl.pallas_call(
        kern,
        grid=(b, t // tp),
        in_specs=[pl.BlockSpec((1, tp, C_B), lambda i, j: (i, j, COL_Q // C_B)),
                  pl.BlockSpec((1, tp, C_B), lambda i, j: (i, j, COL_K // C_B)),
                  pl.BlockSpec((1, tp, C_B), lambda i, j: (i, j, COL_V // C_B)),
                  pl.BlockSpec((1, tp, LANE), lambda i, j: (i, j, COL_F // LANE)),
                  pl.BlockSpec((1, C_B), fix), pl.BlockSpec((1, C_B), fix), pl.BlockSpec((1, LANE), fix)],
        out_specs=[pl.BlockSpec((1, tp, C_B), lambda i, j: (i, j, 0)),
                   pl.BlockSpec((1, tp, LANE), lambda i, j: (i, j, 0)),
                   pl.BlockSpec((1, H_B, tp, LANE), lambda i, j: (i, 0, j, 0)),
                   pl.BlockSpec((1, H_B, tp, LANE), lambda i, j: (i, 0, j, 0)),
                   pl.BlockSpec((1, tp, C_B), lambda i, j: (i, j, 0))],
        out_shape=[jax.ShapeDtypeStruct((b, t, C_B), F32),
                   jax.ShapeDtypeStruct((b, t, LANE), F32),
                   jax.ShapeDtypeStruct((b, H_B, t, LANE), BF16),
                   jax.ShapeDtypeStruct((b, H_B, t, LANE), BF16),
                   jax.ShapeDtypeStruct((b, t, C_B), BF16)],
        scratch_shapes=[pltpu.VMEM((1, LANE), F32)],
        compiler_params=_cparams("parallel", "arbitrary"),
        name="fox_prep",
    )(z3, z3, z3, z3, qn_t, kn_t, bf_pad)


def _fox_attn_kernel(q_ref, k_ref, v_ref, o_ref, m_scr, l_scr, acc_scr, *, tq):
    qi = pl.program_id(2)
    m_scr[...] = jnp.full_like(m_scr, NEG_BIG)
    l_scr[...] = jnp.zeros_like(l_scr)
    acc_scr[...] = jnp.zeros_like(acc_scr)
    lane = _iota((tq, LANE), 1)
    first = lane < HEAD_DIM_B
    qpos = qi * tq + _iota((tq, tq), 0)

    def body(kj, carry):
        rows = pl.ds(pl.multiple_of(kj * tq, tq), tq)
        vs = v_ref[0, rows, :]
        kpos = kj * tq + _iota((tq, tq), 1)
        alphas, pvs = [], []
        for hh in range(2):
            s = lax.dot_general(q_ref[0, hh], k_ref[0, hh, rows, :], (((1,), (1,)), ((), ())),
                                preferred_element_type=F32)
            s = jnp.where(kpos <= qpos, s, -jnp.inf)
            m_old = m_scr[hh]
            m_new = jnp.maximum(m_old, jnp.max(s, axis=1, keepdims=True))
            alpha = jnp.exp(m_old - m_new)
            p = jnp.exp(s - m_new)
            l_scr[hh] = alpha * l_scr[hh] + jnp.sum(p, axis=1, keepdims=True)
            m_scr[hh] = m_new
            alphas.append(alpha)
            pvs.append(jnp.dot(p.astype(BF16), vs, preferred_element_type=F32))
        acc_scr[...] = (jnp.where(first, alphas[0], alphas[1]) * acc_scr[...]
                        + jnp.where(first, pvs[0], pvs[1]))
        return carry

    lax.fori_loop(0, qi + 1, body, 0)
    o_ref[0] = (acc_scr[...] / jnp.where(first, l_scr[0], l_scr[1])).astype(o_ref.dtype)


def _fox_attn(qa, ka, vb, tq):
    b, _, t, _ = qa.shape
    kern = functools.partial(_fox_attn_kernel, tq=tq)
    return pl.pallas_call(
        kern,
        grid=(b, H_B // 2, t // tq),
        in_specs=[pl.BlockSpec((1, 2, tq, LANE), lambda i, h, j: (i, h, j, 0)),
                  pl.BlockSpec((1, 2, t, LANE), lambda i, h, j: (i, h, 0, 0)),
                  pl.BlockSpec((1, t, LANE), lambda i, h, j: (i, 0, h))],
        out_specs=pl.BlockSpec((1, tq, LANE), lambda i, h, j: (i, j, h)),
        out_shape=jax.ShapeDtypeStruct((b, t, C_B), BF16),
        scratch_shapes=[pltpu.VMEM((2, tq, 1), F32), pltpu.VMEM((2, tq, 1), F32),
                        pltpu.VMEM((tq, LANE), F32)],
        compiler_params=_cparams("parallel", "parallel", "arbitrary"),
        name="fox_attn",
    )(qa, ka, vb)


PAGES_PER_STEP = 4
N_ROWS = 32


def _fox_paged_kernel(pt_ref, q_ref, kn_ref, vn_ref, lfn_ref, *rest, n_q, n_steps):
    pp = PAGES_PER_STEP
    k_refs, v_refs, lf_refs = rest[0:pp], rest[pp:2 * pp], rest[2 * pp:3 * pp]
    o_ref = rest[3 * pp]
    qbd_scr, m_scr, l_scr, acc_scr, car_scr, cq_scr = rest[3 * pp + 1:]
    j = pl.program_id(1)
    p_sz = PAGE_SIZE
    r_head = _iota((N_ROWS, 1), 0) // n_q
    r_q = _iota((N_ROWS, 1), 0) % n_q
    expand = (_iota((N_ROWS, H_B), 0) // n_q == _iota((N_ROWS, H_B), 1)).astype(BF16)
    ki = _iota((p_sz, p_sz), 0)
    kj = _iota((p_sz, p_sz), 1)

    def update(s, vt):
        m_old = m_scr[...]
        m_new = jnp.maximum(m_old, jnp.max(s, axis=1, keepdims=True))
        alpha = jnp.exp(m_old - m_new)
        p = jnp.exp(s - m_new)
        l_scr[...] = alpha * l_scr[...] + jnp.sum(p, axis=1, keepdims=True)
        m_scr[...] = m_new
        acc_scr[...] = alpha * acc_scr[...] + _dot_nt(p, vt)

    @pl.when(j == 0)
    def _():
        lane_head = _iota((N_ROWS, C_B), 1) // HEAD_DIM_B
        qbd = jnp.where(lane_head == r_head, q_ref[0], 0.0).astype(BF16)
        qbd_scr[...] = qbd
        m_scr[...] = jnp.full_like(m_scr, NEG_BIG)
        l_scr[...] = jnp.zeros_like(l_scr)
        acc_scr[...] = jnp.zeros_like(acc_scr)
        car_scr[...] = jnp.zeros_like(car_scr)
        lft = _dot_split_rhs(expand, lfn_ref[0])
        ct = _dot_split_lhs(lft, (ki <= kj).astype(BF16))
        key = _iota((N_ROWS, p_sz), 1)
        cq = jnp.sum(jnp.where(key == r_q, ct, 0.0), axis=1, keepdims=True)
        cq_scr[...] = cq
        s = _dot(qbd, kn_ref[0]) + cq - ct
        s = jnp.where(key <= r_q, s, -jnp.inf)
        update(s, vn_ref[0])

    qbd = qbd_scr[...]
    strict = (ki > kj).astype(BF16)
    for i in range(pp):
        lft = _dot_split_rhs(expand, lf_refs[i][...])
        ex = _dot_split_lhs(lft, strict) + car_scr[...]
        car_scr[...] = car_scr[...] + jnp.sum(lft, axis=1, keepdims=True)
        s = _dot(qbd, k_refs[i][...].reshape(C_B, p_sz)) + cq_scr[...] + ex
        update(s, v_refs[i][...].reshape(C_B, p_sz))

    @pl.when(j == n_steps - 1)
    def _():
        lane_head = _iota((N_ROWS, C_B), 1) // HEAD_DIM_B
        o = jnp.where(lane_head == r_head, acc_scr[...] / l_scr[...], 0.0)
        pick = (_iota((8, N_ROWS), 1) % n_q == _iota((8, N_ROWS), 0)).astype(BF16)
        o_ref[0] = jnp.dot(pick, o.astype(BF16), preferred_element_type=F32).astype(o_ref.dtype)


def _fox_paged(page_table, q32, kn_pad, vn_pad, lfn_pad, cache_k, cache_v, cache_lf, layer, n_q):
    db, n_pages = page_table.shape
    pp = PAGES_PER_STEP
    n_steps = n_pages // pp
    kern = functools.partial(_fox_paged_kernel, n_q=n_q, n_steps=n_steps)

    def kv_map(i):
        return lambda b, j, pt: (layer, pt[b, n_pages - 1 - (j * pp + i)], 0, 0, 0)

    def lf_map(i):
        return lambda b, j, pt: (layer, pt[b, n_pages - 1 - (j * pp + i)], 0, 0)

    per_b = lambda b, j, pt: (b, 0, 0)
    kv_specs = [pl.BlockSpec((None, None, H_B, HEAD_DIM_B, PAGE_SIZE), kv_map(i)) for i in range(pp)]
    lf_specs = [pl.BlockSpec((None, None, H_B, PAGE_SIZE), lf_map(i)) for i in range(pp)]
    grid_spec = pltpu.PrefetchScalarGridSpec(
        num_scalar_prefetch=1,
        grid=(db, n_steps),
        in_specs=[pl.BlockSpec((1, N_ROWS, C_B), per_b),
                  pl.BlockSpec((1, C_B, PAGE_SIZE), per_b),
                  pl.BlockSpec((1, C_B, PAGE_SIZE), per_b),
                  pl.BlockSpec((1, H_B, PAGE_SIZE), per_b)] + kv_specs + kv_specs + lf_specs,
        out_specs=pl.BlockSpec((1, 8, C_B), per_b),
        scratch_shapes=[pltpu.VMEM((N_ROWS, C_B), BF16), pltpu.VMEM((N_ROWS, 1), F32),
                        pltpu.VMEM((N_ROWS, 1), F32), pltpu.VMEM((N_ROWS, C_B), F32),
                        pltpu.VMEM((N_ROWS, 1), F32), pltpu.VMEM((N_ROWS, 1), F32)],
    )
    return pl.pallas_call(
        kern,
        grid_spec=grid_spec,
        out_shape=jax.ShapeDtypeStruct((db, 8, C_B), F32),
        compiler_params=_cparams("parallel", "arbitrary"),
        name="fox_paged",
    )(page_table, q32, kn_pad, vn_pad, lfn_pad, *([cache_k] * pp), *([cache_v] * pp), *([cache_lf] * pp))


CONV_PAD = 32
CONV_ROWS = 256


def _conv_kernel(z_ref, buf_ref, w_ref, b_ref, g_ref, beta_ref, y_ref, nb_ref, xp_scr, *, t):
    z = z_ref[0]
    u = z[:, :C_CONV] * _sigmoid(z[:, C_CONV:])
    off = CONV_PAD - (CONV_W - 1)
    xp_scr[off:CONV_PAD, :] = buf_ref[0]
    xp_scr[CONV_PAD:CONV_PAD + t, :] = u
    nb_ref[0] = xp_scr[t + off:t + CONV_PAD, :]
    rt = min(t, CONV_ROWS)
    for r0 in range(0, t, rt):
        acc = jnp.zeros((rt, C_CONV), F32) + b_ref[...]
        for j in range(CONV_W):
            acc = acc + w_ref[j:j + 1, :] * xp_scr[r0 + off + j:r0 + off + j + rt, :]
        mean = jnp.mean(acc, axis=-1, keepdims=True)
        xc = acc - mean
        var = jnp.mean(xc * xc, axis=-1, keepdims=True)
        yn = xc * lax.rsqrt(var + LN_EPS) * g_ref[...] + beta_ref[...]
        y_ref[0, r0:r0 + rt, :] = (yn * _sigmoid(yn)).astype(y_ref.dtype)


def _conv_module(z3, buf, w, bias, ln_g, ln_b, out_dtype=BF16):
    b, t, _ = z3.shape
    fix = lambda i: (0, 0)
    kern = functools.partial(_conv_kernel, t=t)
    return pl.pallas_call(
        kern,
        grid=(b,),
        in_specs=[pl.BlockSpec((1, t, 2 * C_CONV), lambda i: (i, 0, COL_C // (2 * C_CONV))),
                  pl.BlockSpec((1, CONV_W - 1, C_CONV), lambda i: (i, 0, 0)),
                  pl.BlockSpec((CONV_W, C_CONV), fix), pl.BlockSpec((1, C_CONV), fix),
                  pl.BlockSpec((1, C_CONV), fix), pl.BlockSpec((1, C_CONV), fix)],
        out_specs=[pl.BlockSpec((1, t, C_CONV), lambda i: (i, 0, 0)),
                   pl.BlockSpec((1, CONV_W - 1, C_CONV), lambda i: (i, 0, 0))],
        out_shape=[jax.ShapeDtypeStruct((b, t, C_CONV), out_dtype),
                   jax.ShapeDtypeStruct((b, CONV_W - 1, C_CONV), F32)],
        scratch_shapes=[pltpu.VMEM((t + CONV_PAD, C_CONV), F32)],
        compiler_params=_cparams("parallel"),
        name="conv_module",
    )(z3, buf, w, bias.reshape(1, -1), ln_g.reshape(1, -1), ln_b.reshape(1, -1))


def _pre_attn_kernel(x_ref, ya_ref, yb_ref, yc_ref, woa_ref, wob_ref, woc_ref, gx_ref, wq_ref, qn_ref,
                     x1_ref, q_ref):
    x1 = (x_ref[...] + jnp.dot(ya_ref[...], woa_ref[...], preferred_element_type=F32)
          + jnp.dot(yb_ref[...], wob_ref[...], preferred_element_type=F32)
          + jnp.dot(yc_ref[...], woc_ref[...], preferred_element_type=F32))
    x1_ref[...] = x1
    q = jnp.dot(_rms(x1, gx_ref[...]).astype(BF16), wq_ref[...], preferred_element_type=F32)
    for h in range(H_X):
        sl = slice(h * HEAD_DIM_X, (h + 1) * HEAD_DIM_X)
        q_ref[:, sl] = (_rms(q[:, sl], qn_ref[...]) * (HEAD_DIM_X ** -0.5)).astype(q_ref.dtype)


def _pre_attn(x2d, ya, yb, yc, wl, tm):
    m, d = x2d.shape
    row = lambda i: (i, 0)
    fix = lambda i: (0, 0)
    return pl.pallas_call(
        _pre_attn_kernel,
        grid=(m // tm,),
        in_specs=[pl.BlockSpec((tm, d), row), pl.BlockSpec((tm, C_A), row), pl.BlockSpec((tm, C_B), row),
                  pl.BlockSpec((tm, C_CONV), row),
                  pl.BlockSpec((C_A, d), fix), pl.BlockSpec((C_B, d), fix), pl.BlockSpec((C_CONV, d), fix),
                  pl.BlockSpec((1, d), fix), pl.BlockSpec((d, d), fix), pl.BlockSpec((1, HEAD_DIM_X), fix)],
        out_specs=[pl.BlockSpec((tm, d), row), pl.BlockSpec((tm, d), row)],
        out_shape=[jax.ShapeDtypeStruct((m, d), F32), jax.ShapeDtypeStruct((m, d), BF16)],
        compiler_params=_cparams("parallel"),
        name="pre_attn",
    )(x2d, ya, yb, yc, wl['w_out_a'], wl['w_out_b'], wl['w_out_c'], wl['g_x'].reshape(1, d),
      wl['w_xq_bf'], wl['xq_norm'].reshape(1, HEAD_DIM_X))


def _xattn_kernel(q_ref, k_ref, v_ref, o_ref):
    q = q_ref[0]
    kb = k_ref[0].astype(BF16)
    vb = v_ref[0].astype(BF16)
    for h in range(H_X):
        sl = slice(h * HEAD_DIM_X, (h + 1) * HEAD_DIM_X)
        s = lax.dot_general(q[:, sl], kb[:, sl], (((1,), (1,)), ((), ())), preferred_element_type=F32)
        m = jnp.max(s, axis=1, keepdims=True)
        p = jnp.exp(s - m)
        p = p / jnp.sum(p, axis=1, keepdims=True)
        o_ref[0, :, sl] = jnp.dot(p.astype(BF16), vb[:, sl], preferred_element_type=F32).astype(o_ref.dtype)


def _xattn(q3, mk3, mv3, tq):
    b, t, d = q3.shape
    return pl.pallas_call(
        _xattn_kernel,
        grid=(b, t // tq),
        in_specs=[pl.BlockSpec((1, tq, d), lambda i, j: (i, j, 0)),
                  pl.BlockSpec((1, N_MEM, d), lambda i, j: (i, 0, 0)),
                  pl.BlockSpec((1, N_MEM, d), lambda i, j: (i, 0, 0))],
        out_specs=pl.BlockSpec((1, tq, d), lambda i, j: (i, j, 0)),
        out_shape=jax.ShapeDtypeStruct((b, t, d), BF16),
        compiler_params=_cparams("parallel", "parallel"),
        name="xattn",
    )(q3, mk3, mv3)


FFN_TN = 256
FFN_CHUNKS = D_FF // FFN_TN


def _ffn_kernel(*refs, tm, tiles_per_seq, seg, carried):
    if carried:
        (x1_ref, o_ref, wo_ref, g_ref, wg_ref, wv_ref, cwg_ref, cwv_ref, cbg_ref, cbv_ref, wd_ref,
         y_ref, hg_ref, hv_ref, x2_scr, xn_scr, acc_scr, cg_scr, cv_scr) = refs
    else:
        (x1_ref, o_ref, wo_ref, g_ref, wg_ref, wv_ref, cwg_ref, cwv_ref, cbg_ref, cbv_ref, wd_ref,
         p1g_ref, p1v_ref, p2g_ref, p2v_ref,
         y_ref, hg_ref, hv_ref, x2_scr, xn_scr, acc_scr) = refs
    mi = pl.program_id(0)
    c = pl.program_id(1)

    @pl.when(c == 0)
    def _():
        x2 = x1_ref[...] + jnp.dot(o_ref[...], wo_ref[...], preferred_element_type=F32)
        x2_scr[...] = x2
        xn_scr[...] = _rms(x2, g_ref[...]).astype(BF16)
        acc_scr[...] = jnp.zeros_like(acc_scr)

    xn = xn_scr[...]
    row = _iota((tm, 1), 0)
    halves = []
    for name in ("g", "v"):
        w_ref, cw_ref, cb_ref = (wg_ref, cwg_ref, cbg_ref) if name == "g" else (wv_ref, cwv_ref, cbv_ref)
        h = jnp.dot(xn, w_ref[...], preferred_element_type=F32)
        hm1 = pltpu.roll(h, 1, axis=0)
        hm2 = pltpu.roll(h, 2, axis=0)
        if carried:
            c_scr = cg_scr if name == "g" else cv_scr
            prev = jnp.where(mi % tiles_per_seq == 0, 0.0, c_scr[c])
            hm1 = jnp.where(row == 0, prev[7:8, :], hm1)
            hm2 = jnp.where(row == 0, prev[6:7, :], jnp.where(row == 1, prev[7:8, :], hm2))
            c_scr[c] = h[tm - 8:tm, :]
            out_ref = hg_ref if name == "g" else hv_ref
            out_ref[0] = h[tm - 8:tm, :]
        else:
            p1 = (p1g_ref if name == "g" else p1v_ref)[...]
            p2 = (p2g_ref if name == "g" else p2v_ref)[...]
            hm1 = jnp.where(row % seg == 0, p1, hm1)
            hm2 = jnp.where(row % seg < 2, p2, hm2)
            out_ref = hg_ref if name == "g" else hv_ref
            out_ref[...] = h
        halves.append(cw_ref[0:1, :] * hm2 + cw_ref[1:2, :] * hm1 + cw_ref[2:3, :] * h + cb_ref[...])
    gate, val = halves
    act = gate * _sigmoid(gate) * val
    acc_scr[...] += jnp.dot(act.astype(BF16), wd_ref[...], preferred_element_type=F32)

    @pl.when(c == FFN_CHUNKS - 1)
    def _():
        y_ref[...] = x2_scr[...] + acc_scr[...]


def _ffn(x1, o, wl, *, tm, tiles_per_seq=1, seg=1, prev=None):
    m, d = x1.shape
    carried = prev is None
    tn = FFN_TN
    row = lambda i, c: (i, 0)
    fix = lambda i, c: (0, 0)
    gcol = lambda i, c: (0, c)
    vcol = lambda i, c: (0, c + FFN_CHUNKS)
    in_specs = [pl.BlockSpec((tm, d), row), pl.BlockSpec((tm, d), row), pl.BlockSpec((d, d), fix),
                pl.BlockSpec((1, d), fix),
                pl.BlockSpec((d, tn), gcol), pl.BlockSpec((d, tn), vcol),
                pl.BlockSpec((FFN_CONV_W, tn), gcol), pl.BlockSpec((FFN_CONV_W, tn), vcol),
                pl.BlockSpec((1, tn), gcol), pl.BlockSpec((1, tn), vcol),
                pl.BlockSpec((tn, d), lambda i, c: (c, 0))]
    args = [x1, o, wl['w_xo_bf'], wl['g_ffn'].reshape(1, d), wl['w_up_bf'], wl['w_up_bf'],
            wl['ffn_conv_w'], wl['ffn_conv_w'], wl['ffn_conv_b'].reshape(1, -1),
            wl['ffn_conv_b'].reshape(1, -1), wl['w_down_bf']]
    scratch = [pltpu.VMEM((tm, d), F32), pltpu.VMEM((tm, d), BF16), pltpu.VMEM((tm, d), F32)]
    if carried:
        h_shape = jax.ShapeDtypeStruct((m // tm, 8, D_FF), F32)
        h_spec = pl.BlockSpec((1, 8, tn), lambda i, c: (i, 0, c))
        scratch += [pltpu.VMEM((FFN_CHUNKS, 8, tn), F32)] * 2
    else:
        p1, p2 = prev
        in_specs += [pl.BlockSpec((tm, tn), lambda i, c: (i, c)),
                     pl.BlockSpec((tm, tn), lambda i, c: (i, c + FFN_CHUNKS))] * 2
        args += [p1, p1, p2, p2]
        h_shape = jax.ShapeDtypeStruct((m, D_FF), F32)
        h_spec = pl.BlockSpec((tm, tn), lambda i, c: (i, c))
    kern = functools.partial(_ffn_kernel, tm=tm, tiles_per_seq=tiles_per_seq, seg=seg, carried=carried)
    return pl.pallas_call(
        kern,
        grid=(m // tm, FFN_CHUNKS),
        in_specs=in_specs,
        out_specs=[pl.BlockSpec((tm, d), row), h_spec, h_spec],
        out_shape=[jax.ShapeDtypeStruct((m, d), F32), h_shape, h_shape],
        scratch_shapes=scratch,
        compiler_params=_cparams("arbitrary", "arbitrary"),
        name="ffn",
    )(*args)


def _prep_weights(w, l):
    wl = {n: a[l] for n, a in w.items()}
    w_in = wl['w_in']
    o_b = RWKV_COLS
    o_c = RWKV_COLS + 3 * C_B + H_B
    pad = jnp.zeros((D_MODEL, LANE - H_B), F32)
    w_in2 = jnp.concatenate([w_in[:, :o_b], w_in[:, o_b:o_b + 3 * C_B], w_in[:, o_c:],
                             w_in[:, o_b + 3 * C_B:o_c], pad], axis=1)
    wl['w_in_bf'] = w_in2.astype(BF16)
    for n in ('rwkv_w_up', 'rwkv_a_up', 'rwkv_g_up', 'w_xq', 'w_xk', 'w_xv', 'w_xo', 'w_up', 'w_down'):
        wl[n + '_bf'] = wl[n].astype(BF16)
    w_out = wl['w_out'].astype(BF16)
    wl['w_out_a'] = w_out[:C_A]
    wl['w_out_b'] = w_out[C_A:C_A + C_B]
    wl['w_out_c'] = w_out[C_A + C_B:]
    wl['fox_q_norm_t'] = jnp.tile(wl['fox_q_norm'], H_B).reshape(1, C_B)
    wl['fox_k_norm_t'] = jnp.tile(wl['fox_k_norm'], H_B).reshape(1, C_B)
    wl['fox_b_f_pad'] = jnp.concatenate([wl['fox_b_f'], jnp.zeros((LANE - H_B,), F32)]).reshape(1, LANE)
    return wl


def _pick(n, prefs):
    for p in prefs:
        if n % p == 0:
            return p
    return n


def _prompt_layer(x, mem2d, wl):
    b, t, d = x.shape
    m = b * t
    tm = _pick(m, (512, 256, 128, 64, 32, 16, 8))
    mk, mv = _mem_kv(mem2d, wl['g_mem'], wl['w_xk_bf'], wl['w_xv_bf'], wl['xk_norm'],
                     _pick(mem2d.shape[0], (512, 256)))
    z2 = _norm_matmul(x.reshape(m, d), wl['g_mix'], wl['w_in_bf'], tm)
    z3 = z2.reshape(b, t, IN_COLS_PAD)
    tb = _pick(t, (512, 256, 128, 64))
    ya, s_t = _rwkv(z3, jnp.zeros((b, 1, RWKV_COLS), F32), jnp.zeros((b, H_A, HEAD_DIM_A, HEAD_DIM_A), F32),
                    wl, tb=tb, chunk=min(64, tb), t_valid=tb)
    tp = _pick(t, (256, 128, 64))
    k_n, lf, qa, ka, vb = _fox_prep(z3, wl['fox_q_norm_t'], wl['fox_k_norm_t'], wl['fox_b_f_pad'], tp)
    yb = _fox_attn(qa, ka, vb, _pick(t, (256, 128, 64)))
    yc, conv_buf = _conv_module(z3, jnp.zeros((b, CONV_W - 1, C_CONV), F32), wl['conv_w'], wl['conv_b'],
                                wl['conv_ln_g'], wl['conv_ln_b'])
    x1, q = _pre_attn(x.reshape(m, d), ya.reshape(m, C_A), yb.reshape(m, C_B), yc.reshape(m, C_CONV), wl, tm)
    o = _xattn(q.reshape(b, t, d), mk.reshape(b, N_MEM, d), mv.reshape(b, N_MEM, d),
               _pick(t, (512, 256, 128, 64)))
    tf = _pick(t, (1024, 512, 256, 128, 64))
    y, hg, hv = _ffn(x1, o.reshape(m, d), wl, tm=tf, tiles_per_seq=t // tf)
    tiles = t // tf
    ffn_buf = jnp.concatenate([hg.reshape(b, tiles, 8, D_FF)[:, -1, 6:], hv.reshape(b, tiles, 8, D_FF)[:, -1, 6:]],
                              axis=-1)
    st = (k_n.reshape(b, t, H_B, HEAD_DIM_B),
          z3[:, :, COL_V:COL_V + C_B].reshape(b, t, H_B, HEAD_DIM_B),
          lf[:, :, :H_B],
          s_t,
          z3[:, t - 1:t, :RWKV_COLS],
          conv_buf,
          ffn_buf)
    return y.reshape(b, t, d), st, mk.reshape(b, N_MEM, H_X, HEAD_DIM_X), mv.reshape(b, N_MEM, H_X, HEAD_DIM_X)


def _sample_layer(x, wl, layer, cache_k, cache_v, cache_lf, page_table, s0, shift0, conv0, ffn0, mk, mv):
    b, t, d = x.shape
    m = b * t
    z2 = _norm_matmul(x.reshape(m, d), wl['g_mix'], wl['w_in_bf'], m)
    z3 = z2.reshape(b, t, IN_COLS_PAD)
    t_pad = 8
    za = jnp.pad(z3[:, :, :RWKV_COLS], ((0, 0), (0, t_pad - t), (0, 0)))
    ya, s_t = _rwkv(za, shift0, s0, wl, tb=t_pad, chunk=t_pad, t_valid=t, out_dtype=F32)
    ya = ya[:, :t].astype(BF16)
    q, k_n, lf = _fox_norm(z2, wl['fox_q_norm_t'], wl['fox_k_norm_t'], wl['fox_b_f_pad'])
    zv = z3[:, :, COL_V:COL_V + C_B]
    page_t = lambda a: jnp.pad(jnp.swapaxes(a, 1, 2), ((0, 0), (0, 0), (0, PAGE_SIZE - t)))
    q32 = jnp.tile(q.reshape(b, t, C_B), (1, H_B, 1))
    yb = _fox_paged(page_table, q32, page_t(k_n.reshape(b, t, C_B)), page_t(zv),
                    page_t(lf.reshape(b, t, LANE)[:, :, :H_B]), cache_k, cache_v, cache_lf, layer, t)
    yb = yb[:, :t].astype(BF16)
    yc, conv_buf = _conv_module(z3, conv0, wl['conv_w'], wl['conv_b'], wl['conv_ln_g'], wl['conv_ln_b'], F32)
    yc = yc.astype(BF16)
    x1, q = _pre_attn(x.reshape(m, d), ya.reshape(m, C_A), yb.reshape(m, C_B), yc.reshape(m, C_CONV), wl, m)
    t16 = 16
    qp = jnp.pad(q.reshape(b, t, d), ((0, 0), (0, t16 - t), (0, 0)))
    o = _xattn(qp, mk.reshape(b, N_MEM, d), mv.reshape(b, N_MEM, d), t16)[:, :t]
    zrow = jnp.zeros((b, 1, 2 * D_FF), F32)
    p1 = jnp.concatenate([ffn0[:, 1:2], jnp.tile(zrow, (1, t - 1, 1))], axis=1).reshape(m, 2 * D_FF)
    p2 = jnp.concatenate([ffn0, jnp.tile(zrow, (1, t - 2, 1))], axis=1).reshape(m, 2 * D_FF)
    y, hg, hv = _ffn(x1, o.reshape(m, d), wl, tm=m, seg=t, prev=(p1, p2))
    ffn_buf = jnp.concatenate([hg.reshape(b, t, D_FF)[:, t - 2:], hv.reshape(b, t, D_FF)[:, t - 2:]], axis=-1)
    st = (k_n.reshape(b, t, H_B, HEAD_DIM_B), zv.reshape(b, t, H_B, HEAD_DIM_B),
          lf.reshape(b, t, LANE)[:, :, :H_B], s_t, z3[:, t - 1:t, :RWKV_COLS], conv_buf, ffn_buf)
    return y.reshape(b, t, d), st


def kernel(x_prompt, x_sample, mem_prompt, cache_fox_k, cache_fox_v, cache_fox_logf, page_table, state_rwkv, state_rwkv_shift, state_conv, state_ffn, cache_mem_k, cache_mem_v, g_mix, w_in, rwkv_mu, rwkv_w0, rwkv_w_up, rwkv_a0, rwkv_a_up, rwkv_g_up, rwkv_k_k, rwkv_k_a, rwkv_r_k, rwkv_ln_g, rwkv_ln_b, fox_q_norm, fox_k_norm, fox_b_f, conv_w, conv_b, conv_ln_g, conv_ln_b, w_out, g_x, g_mem, w_xq, w_xk, w_xv, xq_norm, xk_norm, w_xo, g_ffn, w_up, ffn_conv_w, ffn_conv_b, w_down):
    w = dict(g_mix=g_mix, w_in=w_in, rwkv_mu=rwkv_mu, rwkv_w0=rwkv_w0, rwkv_w_up=rwkv_w_up, rwkv_a0=rwkv_a0,
             rwkv_a_up=rwkv_a_up, rwkv_g_up=rwkv_g_up, rwkv_k_k=rwkv_k_k, rwkv_k_a=rwkv_k_a, rwkv_r_k=rwkv_r_k,
             rwkv_ln_g=rwkv_ln_g, rwkv_ln_b=rwkv_ln_b, fox_q_norm=fox_q_norm, fox_k_norm=fox_k_norm,
             fox_b_f=fox_b_f, conv_w=conv_w, conv_b=conv_b, conv_ln_g=conv_ln_g, conv_ln_b=conv_ln_b,
             w_out=w_out, g_x=g_x, g_mem=g_mem, w_xq=w_xq, w_xk=w_xk, w_xv=w_xv, xq_norm=xq_norm,
             xk_norm=xk_norm, w_xo=w_xo, g_ffn=g_ffn, w_up=w_up, ffn_conv_w=ffn_conv_w,
             ffn_conv_b=ffn_conv_b, w_down=w_down)
    depth = w_in.shape[0]
    wls = [_prep_weights(w, l) for l in range(depth)]

    b, n_mem, d = mem_prompt.shape
    mem2d = mem_prompt.reshape(b * n_mem, d)
    x = x_prompt
    p_st, p_mk, p_mv = [], [], []
    for l in range(depth):
        x, st, mk, mv = _prompt_layer(x, mem2d, wls[l])
        p_st.append(st)
        p_mk.append(mk)
        p_mv.append(mv)
    y_prompt = x
    p_out = tuple(jnp.stack(f) for f in zip(*p_st))

    ck = jnp.transpose(cache_fox_k, (0, 1, 3, 4, 2))
    cv = jnp.transpose(cache_fox_v, (0, 1, 3, 4, 2))
    clf = jnp.transpose(cache_fox_logf, (0, 1, 3, 2))
    x = x_sample
    s_st = []
    for l in range(depth):
        x, st = _sample_layer(x, wls[l], l, ck, cv, clf, page_table, state_rwkv[l],
                              state_rwkv_shift[l], state_conv[l], state_ffn[l], cache_mem_k[l], cache_mem_v[l])
        s_st.append(st)
    y_sample = x
    s_out = tuple(jnp.stack(f) for f in zip(*s_st))

    return (y_prompt, y_sample) + p_out + (jnp.stack(p_mk), jnp.stack(p_mv)) + s_out
```

```python
import functools

import jax
import jax.numpy as jnp
from jax import lax
from jax.experimental import pallas as pl
from jax.experimental.pallas import tpu as pltpu

F32 = jnp.float32
BF16 = jnp.bfloat16

D_MODEL = 1024
HEAD_DIM_A = 64
C_A = 256
H_A = 4
W_LORA = 64
A_LORA = 64
G_LORA = 128
RWKV_COLS = 3 * C_A + W_LORA + A_LORA + G_LORA
HEAD_DIM_B = 64
C_B = 512
H_B = 8
C_CONV = 256
CONV_W = 31
N_MEM = 256
H_X = 4
HEAD_DIM_X = 256
D_FF = 2816
FFN_CONV_W = 3
PAGE_SIZE = 128
RMS_EPS = 1e-6
LN_EPS = 1e-5
GN_EPS = 64e-5

COL_Q = RWKV_COLS
COL_K = COL_Q + C_B
COL_V = COL_K + C_B
COL_C = COL_V + C_B
COL_F = COL_C + 2 * C_CONV
LANE = 128
IN_COLS_PAD = COL_F + LANE

VMEM_LIMIT_BYTES = 56 * 1024 * 1024
NEG_BIG = -1e30


def _cparams(*sem):
    return pltpu.CompilerParams(dimension_semantics=sem, vmem_limit_bytes=VMEM_LIMIT_BYTES)


def _dot(a, b):
    return jnp.dot(a.astype(BF16), b.astype(BF16), preferred_element_type=F32)


def _dot_nt(a, b):
    return lax.dot_general(a.astype(BF16), b.astype(BF16), (((1,), (1,)), ((), ())),
                           preferred_element_type=F32)


def _dot_tn(a, b):
    return lax.dot_general(a.astype(BF16), b.astype(BF16), (((0,), (0,)), ((), ())),
                           preferred_element_type=F32)


def _split_bf16(x, n):
    parts, r = [], x
    for i in range(n):
        p = r.astype(BF16)
        parts.append(p)
        if i + 1 < n:
            r = r - p.astype(F32)
    return parts


def _dot_split_lhs(x, m_bf, n=3):
    acc = None
    for p in _split_bf16(x, n):
        d = jnp.dot(p, m_bf, preferred_element_type=F32)
        acc = d if acc is None else acc + d
    return acc


def _dot_split_rhs(m_bf, x, n=3):
    acc = None
    for p in _split_bf16(x, n):
        d = jnp.dot(m_bf, p, preferred_element_type=F32)
        acc = d if acc is None else acc + d
    return acc


def _dot_nt_split_rhs(m_bf, x, n=3):
    acc = None
    for p in _split_bf16(x, n):
        d = lax.dot_general(m_bf, p, (((1,), (1,)), ((), ())), preferred_element_type=F32)
        acc = d if acc is None else acc + d
    return acc


def _iota(shape, axis):
    return lax.broadcasted_iota(jnp.int32, shape, axis)


def _block_ones(n, seg):
    return (_iota((n, n), 0) // seg == _iota((n, n), 1) // seg).astype(BF16)


def _sigmoid(x):
    return 1.0 / (1.0 + jnp.exp(-x))


def _softplus(x):
    return jnp.maximum(x, 0.0) + jnp.log(1.0 + jnp.exp(-jnp.abs(x)))


def _rms(x, g):
    ms = jnp.mean(x * x, axis=-1, keepdims=True)
    return x * lax.rsqrt(ms + RMS_EPS) * g


def _norm_matmul_kernel(x_ref, g_ref, w_ref, o_ref):
    xn = _rms(x_ref[...], g_ref[...])
    o_ref[...] = jnp.dot(xn.astype(BF16), w_ref[...], preferred_element_type=F32)


def _norm_matmul(x2d, g, w_bf, tm):
    m, d = x2d.shape
    n = w_bf.shape[1]
    return pl.pallas_call(
        _norm_matmul_kernel,
        grid=(m // tm,),
        in_specs=[pl.BlockSpec((tm, d), lambda i: (i, 0)),
                  pl.BlockSpec((1, d), lambda i: (0, 0)),
                  pl.BlockSpec((d, n), lambda i: (0, 0))],
        out_specs=pl.BlockSpec((tm, n), lambda i: (i, 0)),
        out_shape=jax.ShapeDtypeStruct((m, n), F32),
        compiler_params=_cparams("parallel"),
        name="norm_matmul",
    )(x2d, g.reshape(1, d), w_bf)


def _mem_kv_kernel(m_ref, g_ref, wk_ref, wv_ref, kn_ref, k_out, v_out):
    xn = _rms(m_ref[...], g_ref[...]).astype(BF16)
    kraw = jnp.dot(xn, wk_ref[...], preferred_element_type=F32)
    v_out[...] = jnp.dot(xn, wv_ref[...], preferred_element_type=F32)
    for h in range(H_X):
        sl = slice(h * HEAD_DIM_X, (h + 1) * HEAD_DIM_X)
        k_out[:, sl] = _rms(kraw[:, sl], kn_ref[...])


def _mem_kv(mem2d, g, wk_bf, wv_bf, k_norm, tm):
    m, d = mem2d.shape
    row = lambda i: (i, 0)
    fix = lambda i: (0, 0)
    return pl.pallas_call(
        _mem_kv_kernel,
        grid=(m // tm,),
        in_specs=[pl.BlockSpec((tm, d), row), pl.BlockSpec((1, d), fix),
                  pl.BlockSpec((d, d), fix), pl.BlockSpec((d, d), fix),
                  pl.BlockSpec((1, HEAD_DIM_X), fix)],
        out_specs=[pl.BlockSpec((tm, d), row), pl.BlockSpec((tm, d), row)],
        out_shape=[jax.ShapeDtypeStruct((m, d), F32)] * 2,
        compiler_params=_cparams("parallel"),
        name="mem_kv",
    )(mem2d, g.reshape(1, d), wk_bf, wv_bf, k_norm.reshape(1, HEAD_DIM_X))


RWKV_CHUNK = 64
RWKV_UNROLL = 2

def _rwkv_kernel(z_ref, shift_ref, s0_ref, mu_ref, w0_ref, wup_ref, a0_ref, aup_ref, gup_ref,
                 kk_ref, ka_ref, rk_ref, lng_ref, lnb_ref,
                 y_ref, st_ref,
                 s_scr, zprev_scr, lw_scr, kn_scr, be_scr, km_scr, r_scr, v_scr, yacc_scr,
                 rhat_scr, gend_scr, glr_scr, dadd_scr,
                 *, tb, chunk, t_valid, unroll):
    ti = pl.program_id(1)

    @pl.when(ti == 0)
    def _():
        s_scr[...] = s0_ref[0]
        zprev_scr[...] = shift_ref[0]

    z = z_ref[0]
    row = _iota((tb, 1), 0)
    zprev = jnp.where(row == 0, zprev_scr[...], pltpu.roll(z, 1, axis=0))
    zprev_scr[...] = z[tb - 1:tb, :]
    zs = z + (zprev - z) * mu_ref[...]
    r = zs[:, 0:C_A]
    k = zs[:, C_A:2 * C_A]
    v = zs[:, 2 * C_A:3 * C_A]
    o_w = 3 * C_A
    wd = zs[:, o_w:o_w + W_LORA]
    ad = zs[:, o_w + W_LORA:o_w + W_LORA + A_LORA]
    gd = zs[:, o_w + W_LORA + A_LORA:RWKV_COLS]
    w_log = -_softplus(-(w0_ref[...] + _dot(jnp.tanh(wd), wup_ref[...]))) - 0.5
    lw = -jnp.exp(w_log)
    a = _sigmoid(a0_ref[...] + _dot(ad, aup_ref[...]))
    g = _dot(_sigmoid(gd), gup_ref[...])
    k_mod = k * (1.0 + (a - 1.0) * ka_ref[...])
    bd = _block_ones(C_A, HEAD_DIM_A)
    kk0 = k * kk_ref[...]
    kn = kk0 / jnp.maximum(jnp.sqrt(_dot_split_lhs(kk0 * kk0, bd)), 1e-12)
    if t_valid < tb:
        ok = row < t_valid
        lw = jnp.where(ok, lw, 0.0)
        kn = jnp.where(ok, kn, 0.0)
        k_mod = jnp.where(ok, k_mod, 0.0)
    lw_scr[...] = lw
    kn_scr[...] = kn
    be_scr[...] = kn * a
    km_scr[...] = k_mod
    r_scr[...] = r
    v_scr[...] = v

    c = chunk
    ri = _iota((c, c), 0)
    ci = _iota((c, c), 1)
    lincl = (ri >= ci).astype(BF16)
    m_strict = ri > ci
    m_incl = ri >= ci
    eye = (ri == ci).astype(F32)

    heads = range(H_A)
    hsl = [slice(h * HEAD_DIM_A, (h + 1) * HEAD_DIM_A) for h in heads]

    def chunk_terms(j, carry):
        rows = pl.ds(pl.multiple_of(j * c, c), c)
        lw_c = lw_scr[rows, :]
        cum = _dot_split_rhs(lincl, lw_c)
        g_inc = jnp.exp(cum)
        g_exc = jnp.exp(cum - lw_c)
        g_inv = jnp.exp(-cum)
        g_end = g_inc[c - 1:c, :]
        at = -kn_scr[rows, :] * g_exc
        bt = be_scr[rows, :] * g_inv
        kt = km_scr[rows, :] * g_inv
        rt = r_scr[rows, :] * g_inc
        vv = v_scr[rows, :]
        btc = bt * g_end
        ktc = kt * g_end
        a_ab = [jnp.where(m_strict, _dot_nt(at[:, s], bt[:, s]), 0.0) for s in hsl]
        a_ak = [jnp.where(m_strict, _dot_nt(at[:, s], kt[:, s]), 0.0) for s in hsl]
        m_rb = [jnp.where(m_incl, _dot_nt(rt[:, s], bt[:, s]), 0.0) for s in hsl]
        m_rk = [jnp.where(m_incl, _dot_nt(rt[:, s], kt[:, s]), 0.0) for s in hsl]
        x = [eye + a for a in a_ab]
        p = a_ab
        n = 1
        while 2 * n < c:
            p = [_dot(q, q) for q in p]
            x = [xi + _dot(pi, xi) for xi, pi in zip(x, p)]
            n *= 2
        a_hat = [_dot(x[h], at[:, hsl[h]]) for h in heads]
        u0 = [_dot(x[h], _dot(a_ak[h], vv[:, hsl[h]])) for h in heads]
        r_hat = [rt[:, hsl[h]] + _dot(m_rb[h], a_hat[h]) for h in heads]
        y0 = [_dot(m_rb[h], u0[h]) + _dot(m_rk[h], vv[:, hsl[h]]) for h in heads]
        g_lr = [_dot_tn(a_hat[h], btc[:, hsl[h]]) for h in heads]
        d_add = [_dot_tn(u0[h], btc[:, hsl[h]]) + _dot_tn(vv[:, hsl[h]], ktc[:, hsl[h]]) for h in heads]
        rhat_scr[rows, :] = jnp.concatenate(r_hat, axis=1)
        yacc_scr[rows, :] = jnp.concatenate(y0, axis=1)
        gend_scr[j] = jnp.broadcast_to(g_end, (8, C_A))
        for h in heads:
            glr_scr[j, h] = g_lr[h]
            dadd_scr[j, h] = d_add[h]
        return carry

    lax.fori_loop(0, tb // c, chunk_terms, 0, unroll=min(unroll, tb // c))

    def chunk_state(j, carry):
        rows = pl.ds(pl.multiple_of(j * c, c), c)
        s = [s_scr[h] for h in heads]
        r_hat = rhat_scr[rows, :]
        g_end = gend_scr[j][0:1, :]
        y = [_dot_nt(r_hat[:, hsl[h]], s[h]) for h in heads]
        s_new = [s[h] * g_end[:, hsl[h]] + _dot(s[h], glr_scr[j, h]) + dadd_scr[j, h] for h in heads]
        yacc_scr[rows, :] = yacc_scr[rows, :] + jnp.concatenate(y, axis=1)
        for h in heads:
            s_scr[h] = s_new[h]
        return carry

    lax.fori_loop(0, tb // c, chunk_state, 0)

    y = yacc_scr[...]
    inv_n = 1.0 / HEAD_DIM_A
    mean = _dot_split_lhs(y, bd) * inv_n
    yc = y - mean
    var = _dot_split_lhs(yc * yc, bd) * inv_n
    yn = yc * lax.rsqrt(var + GN_EPS) * lng_ref[...] + lnb_ref[...]
    bonus = _dot_split_lhs(r * k_mod * rk_ref[...], bd) * v
    y_ref[0] = ((yn + bonus) * g).astype(y_ref.dtype)

    @pl.when(ti == pl.num_programs(1) - 1)
    def _():
        st_ref[0] = s_scr[...]


def _rwkv(z3, shift0, s0, wl, *, tb, chunk, t_valid, unroll=1, out_dtype=BF16):
    b, t, _ = z3.shape
    vec = lambda n: pl.BlockSpec((1, n), lambda i, j: (0, 0))
    mat = lambda r, c_: pl.BlockSpec((r, c_), lambda i, j: (0, 0))
    kern = functools.partial(_rwkv_kernel, tb=tb, chunk=chunk, t_valid=t_valid, unroll=unroll)
    nck = tb // chunk
    scr = [pltpu.VMEM((H_A, HEAD_DIM_A, HEAD_DIM_A), F32), pltpu.VMEM((1, RWKV_COLS), F32)]
    scr += [pltpu.VMEM((tb, C_A), F32)] * 8
    scr += [pltpu.VMEM((nck, 8, C_A), F32)]
    scr += [pltpu.VMEM((nck, H_A, HEAD_DIM_A, HEAD_DIM_A), F32)] * 2
    return pl.pallas_call(
        kern,
        grid=(b, t // tb),
        in_specs=[pl.BlockSpec((1, tb, RWKV_COLS), lambda i, j: (i, j, 0)),
                  pl.BlockSpec((1, 1, RWKV_COLS), lambda i, j: (i, 0, 0)),
                  pl.BlockSpec((1, H_A, HEAD_DIM_A, HEAD_DIM_A), lambda i, j: (i, 0, 0, 0)),
                  vec(RWKV_COLS), vec(C_A), mat(W_LORA, C_A), vec(C_A), mat(A_LORA, C_A),
                  mat(G_LORA, C_A), vec(C_A), vec(C_A), vec(C_A), vec(C_A), vec(C_A)],
        out_specs=[pl.BlockSpec((1, tb, C_A), lambda i, j: (i, j, 0)),
                   pl.BlockSpec((1, H_A, HEAD_DIM_A, HEAD_DIM_A), lambda i, j: (i, 0, 0, 0))],
        out_shape=[jax.ShapeDtypeStruct((b, t, C_A), out_dtype),
                   jax.ShapeDtypeStruct((b, H_A, HEAD_DIM_A, HEAD_DIM_A), F32)],
        scratch_shapes=scr,
        compiler_params=_cparams("parallel", "arbitrary"),
        name="rwkv7",
    )(z3, shift0, s0, wl['rwkv_mu'].reshape(1, -1), wl['rwkv_w0'].reshape(1, -1), wl['rwkv_w_up_bf'],
      wl['rwkv_a0'].reshape(1, -1), wl['rwkv_a_up_bf'], wl['rwkv_g_up_bf'],
      wl['rwkv_k_k'].reshape(1, -1), wl['rwkv_k_a'].reshape(1, -1), wl['rwkv_r_k'].reshape(1, -1),
      wl['rwkv_ln_g'].reshape(1, -1), wl['rwkv_ln_b'].reshape(1, -1))


def _fox_norms(zq, zk, fl, qn_g, kn_g, bf):
    bd = _block_ones(C_B, HEAD_DIM_B)
    inv_n = 1.0 / HEAD_DIM_B
    q = zq * lax.rsqrt(_dot_split_lhs(zq * zq, bd, 2) * inv_n + RMS_EPS) * qn_g
    k = zk * lax.rsqrt(_dot_split_lhs(zk * zk, bd, 2) * inv_n + RMS_EPS) * kn_g
    logf = -_softplus(-(fl + bf))
    return q * (HEAD_DIM_B ** -0.5), k, logf


def _fox_norm_kernel(zq_ref, zk_ref, fl_ref, qn_ref, kn_ref, bf_ref, q_out, k_out, lf_out):
    q, k, logf = _fox_norms(zq_ref[...], zk_ref[...], fl_ref[...], qn_ref[...], kn_ref[...], bf_ref[...])
    q_out[...] = q
    k_out[...] = k
    lf_out[...] = logf


def _fox_norm(z2, qn_t, kn_t, bf_pad):
    m = z2.shape[0]
    fix = lambda i: (0, 0)
    return pl.pallas_call(
        _fox_norm_kernel,
        grid=(1,),
        in_specs=[pl.BlockSpec((m, C_B), lambda i: (0, COL_Q // C_B)),
                  pl.BlockSpec((m, C_B), lambda i: (0, COL_K // C_B)),
                  pl.BlockSpec((m, LANE), lambda i: (0, COL_F // LANE)),
                  pl.BlockSpec((1, C_B), fix), pl.BlockSpec((1, C_B), fix), pl.BlockSpec((1, LANE), fix)],
        out_specs=[pl.BlockSpec((m, C_B), fix), pl.BlockSpec((m, C_B), fix), pl.BlockSpec((m, LANE), fix)],
        out_shape=[jax.ShapeDtypeStruct((m, C_B), F32), jax.ShapeDtypeStruct((m, C_B), F32),
                   jax.ShapeDtypeStruct((m, LANE), F32)],
        compiler_params=_cparams("arbitrary"),
        name="fox_norm",
    )(z2, z2, z2, qn_t, kn_t, bf_pad)


N_AUG = 3
FOX_TILE = 512


def _fox_prep_kernel(zq_ref, zk_ref, zv_ref, fl_ref, qn_ref, kn_ref, bf_ref,
                     kt_out, vt_out, lft_out, qat_out, ka_out, vat_out, cum_scr, *, tp):
    ti = pl.program_id(1)

    @pl.when(ti == 0)
    def _():
        cum_scr[...] = jnp.zeros_like(cum_scr)

    q, k, logf = _fox_norms(zq_ref[0], zk_ref[0], fl_ref[0], qn_ref[...], kn_ref[...], bf_ref[...])
    qt = q.T
    kt = k.T
    vt = zv_ref[0].T
    kt_out[0] = kt
    vt_out[0] = vt
    lft_out[0] = logf.T[:H_B, :]
    lincl = (_iota((tp, tp), 0) >= _iota((tp, tp), 1)).astype(BF16)
    cum = _dot_split_rhs(lincl, logf) + cum_scr[...]
    cum_scr[...] = cum[tp - 1:tp, :]
    cumt = cum.T
    lane = _iota((tp, LANE), 1)
    sub = _iota((8, tp), 0)
    ones_rows = jnp.ones((HEAD_DIM_B, tp), F32)
    zero_rows = jnp.zeros((LANE - HEAD_DIM_B - 8, tp), F32)
    for h in range(H_B):
        hs = slice(h * HEAD_DIM_B, (h + 1) * HEAD_DIM_B)
        c_row = jnp.broadcast_to(cumt[h:h + 1, :], (8, tp))
        aug_q = jnp.where((sub >= N_AUG) & (sub < 2 * N_AUG), 1.0, 0.0)
        for i, p in enumerate(_split_bf16(c_row, N_AUG)):
            aug_q = jnp.where(sub == i, p.astype(F32), aug_q)
        qat_out[0, h, 0] = jnp.concatenate([qt[hs, :], aug_q, zero_rows], axis=0).astype(BF16)
        vat_out[0, h, 0] = jnp.concatenate([vt[hs, :], ones_rows], axis=0).astype(BF16)
        c_col = jnp.broadcast_to(cum[:, h:h + 1], (tp, LANE))
        aug_k = jnp.where((lane >= HEAD_DIM_B) & (lane < HEAD_DIM_B + N_AUG), 1.0, 0.0)
        for i, p in enumerate(_split_bf16(c_col, N_AUG)):
            aug_k = jnp.where(lane == HEAD_DIM_B + N_AUG + i, -p.astype(F32), aug_k)
        k2 = k[:, (h // 2) * LANE:(h // 2 + 1) * LANE]
        if h % 2 == 1:
            k2 = pltpu.roll(k2, HEAD_DIM_B, axis=1)
        ka_out[0, h, 0] = jnp.where(lane < HEAD_DIM_B, k2, aug_k).astype(BF16)


def _fox_prep(z3, qn_t, kn_t, bf_pad, tp):
    b, t, _ = z3.shape
    nt = t // tp
    fix = lambda i, j: (0, 0)
    kern = functools.partial(_fox_prep_kernel, tp=tp)
    tr_spec = pl.BlockSpec((1, H_B, 1, LANE, tp), lambda i, j: (i, 0, j, 0, 0))
    return pl.pallas_call(
        kern,
        grid=(b, nt),
        in_specs=[pl.BlockSpec((1, tp, C_B), lambda i, j: (i, j, COL_Q // C_B)),
                  pl.BlockSpec((1, tp, C_B), lambda i, j: (i, j, COL_K // C_B)),
                  pl.BlockSpec((1, tp, C_B), lambda i, j: (i, j, COL_V // C_B)),
                  pl.BlockSpec((1, tp, LANE), lambda i, j: (i, j, COL_F // LANE)),
                  pl.BlockSpec((1, C_B), fix), pl.BlockSpec((1, C_B), fix), pl.BlockSpec((1, LANE), fix)],
        out_specs=[pl.BlockSpec((1, C_B, tp), lambda i, j: (i, 0, j)),
                   pl.BlockSpec((1, C_B, tp), lambda i, j: (i, 0, j)),
                   pl.BlockSpec((1, H_B, tp), lambda i, j: (i, 0, j)),
                   tr_spec,
                   pl.BlockSpec((1, H_B, 1, tp, LANE), lambda i, j: (i, 0, j, 0, 0)),
                   tr_spec],
        out_shape=[jax.ShapeDtypeStruct((b, C_B, t), F32),
                   jax.ShapeDtypeStruct((b, C_B, t), F32),
                   jax.ShapeDtypeStruct((b, H_B, t), F32),
                   jax.ShapeDtypeStruct((b, H_B, nt, LANE, tp), BF16),
                   jax.ShapeDtypeStruct((b, H_B, nt, tp, LANE), BF16),
                   jax.ShapeDtypeStruct((b, H_B, nt, LANE, tp), BF16)],
        scratch_shapes=[pltpu.VMEM((1, LANE), F32)],
        compiler_params=_cparams("parallel", "arbitrary"),
        name="fox_prep",
    )(z3, z3, z3, z3, qn_t, kn_t, bf_pad)


def _fox_attn_kernel(qt_ref, k_ref, vt_ref, o_ref, m0_scr, m1_scr, acc0_scr, acc1_scr, *, tq):
    qi = pl.program_id(2)
    m_scrs = (m0_scr, m1_scr)
    acc_scrs = (acc0_scr, acc1_scr)
    for hh in range(2):
        m_scrs[hh][...] = jnp.full_like(m_scrs[hh], NEG_BIG)
        acc_scrs[hh][...] = jnp.zeros_like(acc_scrs[hh])
    qpos = qi * tq + _iota((tq, tq), 1)
    two = range(2)

    def body(kj, carry):
        keep = kj * tq + _iota((tq, tq), 0) <= qpos
        st = [jnp.dot(k_ref[0, hh, kj], qt_ref[0, hh, 0], preferred_element_type=F32) for hh in two]
        st = [jnp.where(keep, s, -jnp.inf) for s in st]
        m_old = [m_scrs[hh][...] for hh in two]
        m_new = [jnp.maximum(m_old[hh], jnp.max(st[hh], axis=0, keepdims=True)) for hh in two]
        alpha = [jnp.exp(m_old[hh] - m_new[hh]) for hh in two]
        p = [jnp.exp(st[hh] - m_new[hh]).astype(BF16) for hh in two]
        pv = [jnp.dot(vt_ref[0, hh, kj], p[hh], preferred_element_type=F32) for hh in two]
        for hh in two:
            acc_scrs[hh][...] = alpha[hh] * acc_scrs[hh][...] + pv[hh]
            m_scrs[hh][...] = m_new[hh]
        return carry

    lax.fori_loop(0, qi + 1, body, 0)
    ot = [acc_scrs[hh][0:HEAD_DIM_B, :] / acc_scrs[hh][HEAD_DIM_B:HEAD_DIM_B + 1, :] for hh in two]
    o_ref[0] = jnp.concatenate(ot, axis=0).T.astype(o_ref.dtype)


def _fox_attn(qat, ka, vat):
    b, _, nt, _, tq = qat.shape
    kern = functools.partial(_fox_attn_kernel, tq=tq)
    return pl.pallas_call(
        kern,
        grid=(b, H_B // 2, nt),
        in_specs=[pl.BlockSpec((1, 2, 1, LANE, tq), lambda i, h, j: (i, h, j, 0, 0)),
                  pl.BlockSpec((1, 2, nt, tq, LANE), lambda i, h, j: (i, h, 0, 0, 0)),
                  pl.BlockSpec((1, 2, nt, LANE, tq), lambda i, h, j: (i, h, 0, 0, 0))],
        out_specs=pl.BlockSpec((1, tq, LANE), lambda i, h, j: (i, j, h)),
        out_shape=jax.ShapeDtypeStruct((b, nt * tq, C_B), BF16),
        scratch_shapes=[pltpu.VMEM((1, tq), F32), pltpu.VMEM((1, tq), F32),
                        pltpu.VMEM((LANE, tq), F32), pltpu.VMEM((LANE, tq), F32)],
        compiler_params=_cparams("parallel", "parallel", "arbitrary"),
        name="fox_attn",
    )(qat, ka, vat)


PAGES_PER_STEP = 8
N_ROWS = 32
N_GATE_SPLIT = 3


def _fox_paged_kernel(pt_ref, q_ref, kn_ref, vn_ref, lfn_ref, *rest, n_q, n_steps):
    pp = PAGES_PER_STEP
    k_refs, v_refs, lf_refs = rest[0:pp], rest[pp:2 * pp], rest[2 * pp:3 * pp]
    o_ref = rest[3 * pp]
    qbd_scr, m_scr, l_scr, acc_scr, car_scr, cq_scr = rest[3 * pp + 1:]
    j = pl.program_id(1)
    p_sz = PAGE_SIZE
    r_head = _iota((N_ROWS, 1), 0) % H_B
    r_q = _iota((N_ROWS, 1), 0) // H_B
    ki = _iota((p_sz, p_sz), 0)
    kj = _iota((p_sz, p_sz), 1)
    per_query = lambda a: jnp.concatenate([a] * n_q, axis=0)

    def update(s_pages, vt_pages):
        s = jnp.concatenate(s_pages, axis=1)
        m_old = m_scr[...]
        m_new = jnp.maximum(m_old, jnp.max(s, axis=1, keepdims=True))
        alpha = jnp.exp(m_old - m_new)
        p = jnp.exp(s - m_new)
        l_scr[...] = alpha * l_scr[...] + jnp.sum(p, axis=1, keepdims=True)
        m_scr[...] = m_new
        pv = None
        for i, vt in enumerate(vt_pages):
            d = _dot_nt(p[:, i * p_sz:(i + 1) * p_sz], vt)
            pv = d if pv is None else pv + d
        acc_scr[...] = alpha * acc_scr[...] + pv

    @pl.when(j == 0)
    def _():
        lane_head = _iota((N_ROWS, C_B), 1) // HEAD_DIM_B
        qbd = jnp.where(lane_head == r_head, q_ref[0], 0.0).astype(BF16)
        qbd_scr[...] = qbd
        m_scr[...] = jnp.full_like(m_scr, NEG_BIG)
        l_scr[...] = jnp.zeros_like(l_scr)
        acc_scr[...] = jnp.zeros_like(acc_scr)
        car_scr[...] = jnp.zeros_like(car_scr)
        lft = per_query(lfn_ref[0])
        ct = _dot_split_lhs(lft, (ki <= kj).astype(BF16))
        key = _iota((N_ROWS, p_sz), 1)
        cq = jnp.sum(jnp.where(key == r_q, ct, 0.0), axis=1, keepdims=True)
        cq_scr[...] = cq
        s = _dot(qbd, kn_ref[0]) + cq - ct
        s = jnp.where(key <= r_q, s, -jnp.inf)
        update([s], [vn_ref[0]])

    pieces = []
    for i in range(pp):
        pieces += [p.astype(F32) for p in _split_bf16(lf_refs[i][...], N_GATE_SPLIT)]
    suffix_and_total = jnp.concatenate([(ki > kj).astype(BF16), jnp.ones((p_sz, p_sz), BF16)], axis=1)
    res = jnp.dot(jnp.concatenate(pieces, axis=0).astype(BF16), suffix_and_total,
                  preferred_element_type=F32)
    qbd = qbd_scr[...]
    cq = cq_scr[...]
    after = car_scr[...]
    s_pages, vt_pages = [], []
    for i in range(pp):
        r0 = i * N_GATE_SPLIT * H_B
        both = res[r0:r0 + H_B, :]
        for n in range(1, N_GATE_SPLIT):
            both = both + res[r0 + n * H_B:r0 + (n + 1) * H_B, :]
        ex = both[:, :p_sz] + after
        after = after + both[:, p_sz:]
        s_pages.append(_dot(qbd, k_refs[i][...].reshape(C_B, p_sz)) + cq + per_query(ex))
        vt_pages.append(v_refs[i][...].reshape(C_B, p_sz))
    car_scr[...] = after
    update(s_pages, vt_pages)

    @pl.when(j == n_steps - 1)
    def _():
        lane_head = _iota((N_ROWS, C_B), 1) // HEAD_DIM_B
        o = jnp.where(lane_head == r_head, acc_scr[...] / l_scr[...], 0.0)
        o_ref[0] = jnp.sum(o.reshape(n_q, H_B, C_B), axis=1)


def _fox_paged(page_table, q32, kn_pad, vn_pad, lfn_pad, cache_k, cache_v, cache_lf, layer, n_q):
    db, n_pages = page_table.shape
    pp = PAGES_PER_STEP
    n_steps = n_pages // pp
    kern = functools.partial(_fox_paged_kernel, n_q=n_q, n_steps=n_steps)

    def kv_map(i):
        return lambda b, j, pt: (layer, pt[b, n_pages - 1 - (j * pp + i)], 0, 0, 0)

    def lf_map(i):
        return lambda b, j, pt: (layer, pt[b, n_pages - 1 - (j * pp + i)], 0, 0)

    per_b = lambda b, j, pt: (b, 0, 0)
    kv_specs = [pl.BlockSpec((None, None, H_B, HEAD_DIM_B, PAGE_SIZE), kv_map(i)) for i in range(pp)]
    lf_specs = [pl.BlockSpec((None, None, H_B, PAGE_SIZE), lf_map(i)) for i in range(pp)]
    grid_spec = pltpu.PrefetchScalarGridSpec(
        num_scalar_prefetch=1,
        grid=(db, n_steps),
        in_specs=[pl.BlockSpec((1, N_ROWS, C_B), per_b),
                  pl.BlockSpec((1, C_B, PAGE_SIZE), per_b),
                  pl.BlockSpec((1, C_B, PAGE_SIZE), per_b),
                  pl.BlockSpec((1, H_B, PAGE_SIZE), per_b)] + kv_specs + kv_specs + lf_specs,
        out_specs=pl.BlockSpec((1, n_q, C_B), per_b),
        scratch_shapes=[pltpu.VMEM((N_ROWS, C_B), BF16), pltpu.VMEM((N_ROWS, 1), F32),
                        pltpu.VMEM((N_ROWS, 1), F32), pltpu.VMEM((N_ROWS, C_B), F32),
                        pltpu.VMEM((H_B, PAGE_SIZE), F32), pltpu.VMEM((N_ROWS, 1), F32)],
    )
    return pl.pallas_call(
        kern,
        grid_spec=grid_spec,
        out_shape=jax.ShapeDtypeStruct((db, n_q, C_B), F32),
        compiler_params=_cparams("parallel", "arbitrary"),
        name="fox_paged",
    )(page_table, q32, kn_pad, vn_pad, lfn_pad, *([cache_k] * pp), *([cache_v] * pp), *([cache_lf] * pp))


CONV_PAD = 32
CONV_ROWS = 256


def _conv_kernel(z_ref, buf_ref, w_ref, b_ref, g_ref, beta_ref, y_ref, nb_ref, xp_scr, *, t):
    z = z_ref[0]
    u = z[:, :C_CONV] * _sigmoid(z[:, C_CONV:])
    off = CONV_PAD - (CONV_W - 1)
    xp_scr[off:CONV_PAD, :] = buf_ref[0]
    xp_scr[CONV_PAD:CONV_PAD + t, :] = u
    nb_ref[0] = xp_scr[t + off:t + CONV_PAD, :]
    rt = min(t, CONV_ROWS)
    for r0 in range(0, t, rt):
        acc = jnp.zeros((rt, C_CONV), F32) + b_ref[...]
        for j in range(CONV_W):
            acc = acc + w_ref[j:j + 1, :] * xp_scr[r0 + off + j:r0 + off + j + rt, :]
        mean = jnp.mean(acc, axis=-1, keepdims=True)
        xc = acc - mean
        var = jnp.mean(xc * xc, axis=-1, keepdims=True)
        yn = xc * lax.rsqrt(var + LN_EPS) * g_ref[...] + beta_ref[...]
        y_ref[0, r0:r0 + rt, :] = (yn * _sigmoid(yn)).astype(y_ref.dtype)


def _conv_module(z3, buf, w, bias, ln_g, ln_b, out_dtype=BF16):
    b, t, _ = z3.shape
    fix = lambda i: (0, 0)
    kern = functools.partial(_conv_kernel, t=t)
    return pl.pallas_call(
        kern,
        grid=(b,),
        in_specs=[pl.BlockSpec((1, t, 2 * C_CONV), lambda i: (i, 0, COL_C // (2 * C_CONV))),
                  pl.BlockSpec((1, CONV_W - 1, C_CONV), lambda i: (i, 0, 0)),
                  pl.BlockSpec((CONV_W, C_CONV), fix), pl.BlockSpec((1, C_CONV), fix),
                  pl.BlockSpec((1, C_CONV), fix), pl.BlockSpec((1, C_CONV), fix)],
        out_specs=[pl.BlockSpec((1, t, C_CONV), lambda i: (i, 0, 0)),
                   pl.BlockSpec((1, CONV_W - 1, C_CONV), lambda i: (i, 0, 0))],
        out_shape=[jax.ShapeDtypeStruct((b, t, C_CONV), out_dtype),
                   jax.ShapeDtypeStruct((b, CONV_W - 1, C_CONV), F32)],
        scratch_shapes=[pltpu.VMEM((t + CONV_PAD, C_CONV), F32)],
        compiler_params=_cparams("parallel"),
        name="conv_module",
    )(z3, buf, w, bias.reshape(1, -1), ln_g.reshape(1, -1), ln_b.reshape(1, -1))


def _pre_attn_kernel(x_ref, ya_ref, yb_ref, yc_ref, woa_ref, wob_ref, woc_ref, gx_ref, wq_ref, qn_ref,
                     x1_ref, q_ref):
    x1 = (x_ref[...] + jnp.dot(ya_ref[...], woa_ref[...], preferred_element_type=F32)
          + jnp.dot(yb_ref[...], wob_ref[...], preferred_element_type=F32)
          + jnp.dot(yc_ref[...], woc_ref[...], preferred_element_type=F32))
    x1_ref[...] = x1
    q = jnp.dot(_rms(x1, gx_ref[...]).astype(BF16), wq_ref[...], preferred_element_type=F32)
    for h in range(H_X):
        sl = slice(h * HEAD_DIM_X, (h + 1) * HEAD_DIM_X)
        q_ref[:, sl] = (_rms(q[:, sl], qn_ref[...]) * (HEAD_DIM_X ** -0.5)).astype(q_ref.dtype)


def _pre_attn(x2d, ya, yb, yc, wl, tm):
    m, d = x2d.shape
    row = lambda i: (i, 0)
    fix = lambda i: (0, 0)
    return pl.pallas_call(
        _pre_attn_kernel,
        grid=(m // tm,),
        in_specs=[pl.BlockSpec((tm, d), row), pl.BlockSpec((tm, C_A), row), pl.BlockSpec((tm, C_B), row),
                  pl.BlockSpec((tm, C_CONV), row),
                  pl.BlockSpec((C_A, d), fix), pl.BlockSpec((C_B, d), fix), pl.BlockSpec((C_CONV, d), fix),
                  pl.BlockSpec((1, d), fix), pl.BlockSpec((d, d), fix), pl.BlockSpec((1, HEAD_DIM_X), fix)],
        out_specs=[pl.BlockSpec((tm, d), row), pl.BlockSpec((tm, d), row)],
        out_shape=[jax.ShapeDtypeStruct((m, d), F32), jax.ShapeDtypeStruct((m, d), BF16)],
        compiler_params=_cparams("parallel"),
        name="pre_attn",
    )(x2d, ya, yb, yc, wl['w_out_a'], wl['w_out_b'], wl['w_out_c'], wl['g_x'].reshape(1, d),
      wl['w_xq_bf'], wl['xq_norm'].reshape(1, HEAD_DIM_X))


MEM_LANE_CHUNKS = HEAD_DIM_X // LANE
MEM_ROWS_PER_TOKEN = H_X * MEM_LANE_CHUNKS


def _xattn_kernel(q_ref, k_ref, v_ref, o_ref, *, row_major):
    q = q_ref[0]

    def head_of(ref, h):
        if row_major:
            return ref[0, :, h * HEAD_DIM_X:(h + 1) * HEAD_DIM_X].astype(BF16)
        parts = [ref[0, pl.ds(c * H_X + h, N_MEM, stride=MEM_ROWS_PER_TOKEN), :]
                 for c in range(MEM_LANE_CHUNKS)]
        return jnp.concatenate(parts, axis=1).astype(BF16)

    for h in range(H_X):
        sl = slice(h * HEAD_DIM_X, (h + 1) * HEAD_DIM_X)
        s = lax.dot_general(q[:, sl], head_of(k_ref, h), (((1,), (1,)), ((), ())),
                            preferred_element_type=F32)
        m = jnp.max(s, axis=1, keepdims=True)
        p = jnp.exp(s - m)
        p = p / jnp.sum(p, axis=1, keepdims=True)
        o_ref[0, :, sl] = jnp.dot(p.astype(BF16), head_of(v_ref, h),
                                  preferred_element_type=F32).astype(o_ref.dtype)


def _xattn(q3, mk3, mv3, tq, row_major=True):
    b, t, d = q3.shape
    kv_block = (1,) + mk3.shape[1:]
    return pl.pallas_call(
        functools.partial(_xattn_kernel, row_major=row_major),
        grid=(b, t // tq),
        in_specs=[pl.BlockSpec((1, tq, d), lambda i, j: (i, j, 0)),
                  pl.BlockSpec(kv_block, lambda i, j: (i, 0, 0)),
                  pl.BlockSpec(kv_block, lambda i, j: (i, 0, 0))],
        out_specs=pl.BlockSpec((1, tq, d), lambda i, j: (i, j, 0)),
        out_shape=jax.ShapeDtypeStruct((b, t, d), BF16),
        compiler_params=_cparams("parallel", "parallel"),
        name="xattn",
    )(q3, mk3, mv3)


def _mem_device_rows(a):
    b = a.shape[0]
    a = a.reshape(b, N_MEM, H_X, MEM_LANE_CHUNKS, LANE)
    return jnp.transpose(a, (0, 1, 3, 2, 4)).reshape(b, N_MEM * MEM_ROWS_PER_TOKEN, LANE)


FFN_TN = 256
FFN_CHUNKS = D_FF // FFN_TN


def _ffn_kernel(*refs, tm, tiles_per_seq, seg, carried):
    if carried:
        (x1_ref, o_ref, wo_ref, g_ref, wg_ref, wv_ref, cwg_ref, cwv_ref, cbg_ref, cbv_ref, wd_ref,
         y_ref, hg_ref, hv_ref, x2_scr, xn_scr, acc_scr, cg_scr, cv_scr) = refs
    else:
        (x1_ref, o_ref, wo_ref, g_ref, wg_ref, wv_ref, cwg_ref, cwv_ref, cbg_ref, cbv_ref, wd_ref,
         p1g_ref, p1v_ref, p2g_ref, p2v_ref,
         y_ref, hg_ref, hv_ref, x2_scr, xn_scr, acc_scr) = refs
    mi = pl.program_id(0)
    c = pl.program_id(1)

    @pl.when(c == 0)
    def _():
        x2 = x1_ref[...] + jnp.dot(o_ref[...], wo_ref[...], preferred_element_type=F32)
        x2_scr[...] = x2
        xn_scr[...] = _rms(x2, g_ref[...]).astype(BF16)
        acc_scr[...] = jnp.zeros_like(acc_scr)

    xn = xn_scr[...]
    row = _iota((tm, 1), 0)
    halves = []
    for name in ("g", "v"):
        w_ref, cw_ref, cb_ref = (wg_ref, cwg_ref, cbg_ref) if name == "g" else (wv_ref, cwv_ref, cbv_ref)
        h = jnp.dot(xn, w_ref[...], preferred_element_type=F32)
        hm1 = pltpu.roll(h, 1, axis=0)
        hm2 = pltpu.roll(h, 2, axis=0)
        if carried:
            c_scr = cg_scr if name == "g" else cv_scr
            prev = jnp.where(mi % tiles_per_seq == 0, 0.0, c_scr[c])
            hm1 = jnp.where(row == 0, prev[7:8, :], hm1)
            hm2 = jnp.where(row == 0, prev[6:7, :], jnp.where(row == 1, prev[7:8, :], hm2))
            c_scr[c] = h[tm - 8:tm, :]
            out_ref = hg_ref if name == "g" else hv_ref
            out_ref[0] = h[tm - 8:tm, :]
        else:
            p1 = (p1g_ref if name == "g" else p1v_ref)[...]
            p2 = (p2g_ref if name == "g" else p2v_ref)[...]
            hm1 = jnp.where(row % seg == 0, p1, hm1)
            hm2 = jnp.where(row % seg < 2, p2, hm2)
            out_ref = hg_ref if name == "g" else hv_ref
            out_ref[...] = h
        halves.append(cw_ref[0:1, :] * hm2 + cw_ref[1:2, :] * hm1 + cw_ref[2:3, :] * h + cb_ref[...])
    gate, val = halves
    act = gate * _sigmoid(gate) * val
    acc_scr[...] += jnp.dot(act.astype(BF16), wd_ref[...], preferred_element_type=F32)

    @pl.when(c == FFN_CHUNKS - 1)
    def _():
        y_ref[...] = x2_scr[...] + acc_scr[...]


def _ffn(x1, o, wl, *, tm, tiles_per_seq=1, seg=1, prev=None):
    m, d = x1.shape
    carried = prev is None
    tn = FFN_TN
    row = lambda i, c: (i, 0)
    fix = lambda i, c: (0, 0)
    gcol = lambda i, c: (0, c)
    vcol = lambda i, c: (0, c + FFN_CHUNKS)
    in_specs = [pl.BlockSpec((tm, d), row), pl.BlockSpec((tm, d), row), pl.BlockSpec((d, d), fix),
                pl.BlockSpec((1, d), fix),
                pl.BlockSpec((d, tn), gcol), pl.BlockSpec((d, tn), vcol),
                pl.BlockSpec((FFN_CONV_W, tn), gcol), pl.BlockSpec((FFN_CONV_W, tn), vcol),
                pl.BlockSpec((1, tn), gcol), pl.BlockSpec((1, tn), vcol),
                pl.BlockSpec((tn, d), lambda i, c: (c, 0))]
    args = [x1, o, wl['w_xo_bf'], wl['g_ffn'].reshape(1, d), wl['w_up_bf'], wl['w_up_bf'],
            wl['ffn_conv_w'], wl['ffn_conv_w'], wl['ffn_conv_b'].reshape(1, -1),
            wl['ffn_conv_b'].reshape(1, -1), wl['w_down_bf']]
    scratch = [pltpu.VMEM((tm, d), F32), pltpu.VMEM((tm, d), BF16), pltpu.VMEM((tm, d), F32)]
    if carried:
        h_shape = jax.ShapeDtypeStruct((m // tm, 8, D_FF), F32)
        h_spec = pl.BlockSpec((1, 8, tn), lambda i, c: (i, 0, c))
        scratch += [pltpu.VMEM((FFN_CHUNKS, 8, tn), F32)] * 2
    else:
        p1, p2 = prev
        in_specs += [pl.BlockSpec((tm, tn), lambda i, c: (i, c)),
                     pl.BlockSpec((tm, tn), lambda i, c: (i, c + FFN_CHUNKS))] * 2
        args += [p1, p1, p2, p2]
        h_shape = jax.ShapeDtypeStruct((m, D_FF), F32)
        h_spec = pl.BlockSpec((tm, tn), lambda i, c: (i, c))
    kern = functools.partial(_ffn_kernel, tm=tm, tiles_per_seq=tiles_per_seq, seg=seg, carried=carried)
    return pl.pallas_call(
        kern,
        grid=(m // tm, FFN_CHUNKS),
        in_specs=in_specs,
        out_specs=[pl.BlockSpec((tm, d), row), h_spec, h_spec],
        out_shape=[jax.ShapeDtypeStruct((m, d), F32), h_shape, h_shape],
        scratch_shapes=scratch,
        compiler_params=_cparams("arbitrary", "arbitrary"),
        name="ffn",
    )(*args)


def _prep_weights(w, l):
    wl = {n: a[l] for n, a in w.items()}
    w_in = wl['w_in']
    o_b = RWKV_COLS
    o_c = RWKV_COLS + 3 * C_B + H_B
    pad = jnp.zeros((D_MODEL, LANE - H_B), F32)
    w_in2 = jnp.concatenate([w_in[:, :o_b], w_in[:, o_b:o_b + 3 * C_B], w_in[:, o_c:],
                             w_in[:, o_b + 3 * C_B:o_c], pad], axis=1)
    wl['w_in_bf'] = w_in2.astype(BF16)
    for n in ('rwkv_w_up', 'rwkv_a_up', 'rwkv_g_up', 'w_xq', 'w_xk', 'w_xv', 'w_xo', 'w_up', 'w_down'):
        wl[n + '_bf'] = wl[n].astype(BF16)
    w_out = wl['w_out'].astype(BF16)
    wl['w_out_a'] = w_out[:C_A]
    wl['w_out_b'] = w_out[C_A:C_A + C_B]
    wl['w_out_c'] = w_out[C_A + C_B:]
    wl['fox_q_norm_t'] = jnp.tile(wl['fox_q_norm'], H_B).reshape(1, C_B)
    wl['fox_k_norm_t'] = jnp.tile(wl['fox_k_norm'], H_B).reshape(1, C_B)
    wl['fox_b_f_pad'] = jnp.concatenate([wl['fox_b_f'], jnp.zeros((LANE - H_B,), F32)]).reshape(1, LANE)
    return wl


def _pick(n, prefs):
    for p in prefs:
        if n % p == 0:
            return p
    return n


def _prompt_layer(x, mem2d, wl):
    b, t, d = x.shape
    m = b * t
    tm = _pick(m, (512, 256, 128, 64, 32, 16, 8))
    mk, mv = _mem_kv(mem2d, wl['g_mem'], wl['w_xk_bf'], wl['w_xv_bf'], wl['xk_norm'],
                     _pick(mem2d.shape[0], (512, 256)))
    z2 = _norm_matmul(x.reshape(m, d), wl['g_mix'], wl['w_in_bf'], tm)
    z3 = z2.reshape(b, t, IN_COLS_PAD)
    tb = _pick(t, (512, 256, 128, 64))
    ya, s_t = _rwkv(z3, jnp.zeros((b, 1, RWKV_COLS), F32), jnp.zeros((b, H_A, HEAD_DIM_A, HEAD_DIM_A), F32),
                    wl, tb=tb, chunk=min(RWKV_CHUNK, tb), t_valid=tb, unroll=RWKV_UNROLL)
    tp = _pick(t, (FOX_TILE, 256, 128))
    kt, vt, lft, qat, ka, vat = _fox_prep(z3, wl['fox_q_norm_t'], wl['fox_k_norm_t'], wl['fox_b_f_pad'], tp)
    yb = _fox_attn(qat, ka, vat)
    yc, conv_buf = _conv_module(z3, jnp.zeros((b, CONV_W - 1, C_CONV), F32), wl['conv_w'], wl['conv_b'],
                                wl['conv_ln_g'], wl['conv_ln_b'])
    x1, q = _pre_attn(x.reshape(m, d), ya.reshape(m, C_A), yb.reshape(m, C_B), yc.reshape(m, C_CONV), wl, tm)
    o = _xattn(q.reshape(b, t, d), mk.reshape(b, N_MEM, d), mv.reshape(b, N_MEM, d),
               _pick(t, (512, 256, 128, 64)))
    tf = _pick(t, (1024, 512, 256, 128, 64))
    y, hg, hv = _ffn(x1, o.reshape(m, d), wl, tm=tf, tiles_per_seq=t // tf)
    tiles = t // tf
    ffn_buf = jnp.concatenate([hg.reshape(b, tiles, 8, D_FF)[:, -1, 6:], hv.reshape(b, tiles, 8, D_FF)[:, -1, 6:]],
                              axis=-1)
    st = (jnp.transpose(kt.reshape(b, H_B, HEAD_DIM_B, t), (0, 3, 1, 2)),
          jnp.transpose(vt.reshape(b, H_B, HEAD_DIM_B, t), (0, 3, 1, 2)),
          jnp.transpose(lft, (0, 2, 1)),
          s_t,
          z3[:, t - 1:t, :RWKV_COLS],
          conv_buf,
          ffn_buf)
    return y.reshape(b, t, d), st, mk.reshape(b, N_MEM, H_X, HEAD_DIM_X), mv.reshape(b, N_MEM, H_X, HEAD_DIM_X)


def _sample_layer(x, wl, layer, cache_k, cache_v, cache_lf, page_table, s0, shift0, conv0, ffn0, mk, mv):
    b, t, d = x.shape
    m = b * t
    z2 = _norm_matmul(x.reshape(m, d), wl['g_mix'], wl['w_in_bf'], m)
    z3 = z2.reshape(b, t, IN_COLS_PAD)
    t_pad = 8
    za = jnp.pad(z3[:, :, :RWKV_COLS], ((0, 0), (0, t_pad - t), (0, 0)))
    ya, s_t = _rwkv(za, shift0, s0, wl, tb=t_pad, chunk=t_pad, t_valid=t, out_dtype=F32)
    ya = ya[:, :t].astype(BF16)
    q, k_n, lf = _fox_norm(z2, wl['fox_q_norm_t'], wl['fox_k_norm_t'], wl['fox_b_f_pad'])
    zv = z3[:, :, COL_V:COL_V + C_B]
    page_t = lambda a: jnp.pad(jnp.swapaxes(a, 1, 2), ((0, 0), (0, 0), (0, PAGE_SIZE - t)))
    q32 = jnp.repeat(q.reshape(b, t, C_B), H_B, axis=1)
    yb = _fox_paged(page_table, q32, page_t(k_n.reshape(b, t, C_B)), page_t(zv),
                    page_t(lf.reshape(b, t, LANE)[:, :, :H_B]), cache_k, cache_v, cache_lf, layer, t)
    yb = yb.astype(BF16)
    yc, conv_buf = _conv_module(z3, conv0, wl['conv_w'], wl['conv_b'], wl['conv_ln_g'], wl['conv_ln_b'], F32)
    yc = yc.astype(BF16)
    x1, q = _pre_attn(x.reshape(m, d), ya.reshape(m, C_A), yb.reshape(m, C_B), yc.reshape(m, C_CONV), wl, m)
    t16 = 16
    qp = jnp.pad(q.reshape(b, t, d), ((0, 0), (0, t16 - t), (0, 0)))
    o = _xattn(qp, _mem_device_rows(mk), _mem_device_rows(mv), t16, row_major=False)[:, :t]
    zrow = jnp.zeros((b, 1, 2 * D_FF), F32)
    p1 = jnp.concatenate([ffn0[:, 1:2], jnp.tile(zrow, (1, t - 1, 1))], axis=1).reshape(m, 2 * D_FF)
    p2 = jnp.concatenate([ffn0, jnp.tile(zrow, (1, t - 2, 1))], axis=1).reshape(m, 2 * D_FF)
    y, hg, hv = _ffn(x1, o.reshape(m, d), wl, tm=m, seg=t, prev=(p1, p2))
    ffn_buf = jnp.concatenate([hg.reshape(b, t, D_FF)[:, t - 2:], hv.reshape(b, t, D_FF)[:, t - 2:]], axis=-1)
    st = (k_n.reshape(b, t, H_B, HEAD_DIM_B), zv.reshape(b, t, H_B, HEAD_DIM_B),
          lf.reshape(b, t, LANE)[:, :, :H_B], s_t, z3[:, t - 1:t, :RWKV_COLS], conv_buf, ffn_buf)
    return y.reshape(b, t, d), st


def kernel(x_prompt, x_sample, mem_prompt, cache_fox_k, cache_fox_v, cache_fox_logf, page_table, state_rwkv, state_rwkv_shift, state_conv, state_ffn, cache_mem_k, cache_mem_v, g_mix, w_in, rwkv_mu, rwkv_w0, rwkv_w_up, rwkv_a0, rwkv_a_up, rwkv_g_up, rwkv_k_k, rwkv_k_a, rwkv_r_k, rwkv_ln_g, rwkv_ln_b, fox_q_norm, fox_k_norm, fox_b_f, conv_w, conv_b, conv_ln_g, conv_ln_b, w_out, g_x, g_mem, w_xq, w_xk, w_xv, xq_norm, xk_norm, w_xo, g_ffn, w_up, ffn_conv_w, ffn_conv_b, w_down):
    w = dict(g_mix=g_mix, w_in=w_in, rwkv_mu=rwkv_mu, rwkv_w0=rwkv_w0, rwkv_w_up=rwkv_w_up, rwkv_a0=rwkv_a0,
             rwkv_a_up=rwkv_a_up, rwkv_g_up=rwkv_g_up, rwkv_k_k=rwkv_k_k, rwkv_k_a=rwkv_k_a, rwkv_r_k=rwkv_r_k,
             rwkv_ln_g=rwkv_ln_g, rwkv_ln_b=rwkv_ln_b, fox_q_norm=fox_q_norm, fox_k_norm=fox_k_norm,
             fox_b_f=fox_b_f, conv_w=conv_w, conv_b=conv_b, conv_ln_g=conv_ln_g, conv_ln_b=conv_ln_b,
             w_out=w_out, g_x=g_x, g_mem=g_mem, w_xq=w_xq, w_xk=w_xk, w_xv=w_xv, xq_norm=xq_norm,
             xk_norm=xk_norm, w_xo=w_xo, g_ffn=g_ffn, w_up=w_up, ffn_conv_w=ffn_conv_w,
             ffn_conv_b=ffn_conv_b, w_down=w_down)
    depth = w_in.shape[0]
    wls = [_prep_weights(w, l) for l in range(depth)]

    b, n_mem, d = mem_prompt.shape
    mem2d = mem_prompt.reshape(b * n_mem, d)
    x = x_prompt
    p_st, p_mk, p_mv = [], [], []
    for l in range(depth):
        x, st, mk, mv = _prompt_layer(x, mem2d, wls[l])
        p_st.append(st)
        p_mk.append(mk)
        p_mv.append(mv)
    y_prompt = x
    p_out = tuple(jnp.stack(f) for f in zip(*p_st))

    ck = jnp.transpose(cache_fox_k, (0, 1, 3, 4, 2))
    cv = jnp.transpose(cache_fox_v, (0, 1, 3, 4, 2))
    clf = jnp.transpose(cache_fox_logf, (0, 1, 3, 2))
    x = x_sample
    s_st = []
    for l in range(depth):
        x, st = _sample_layer(x, wls[l], l, ck, cv, clf, page_table, state_rwkv[l],
                              state_rwkv_shift[l], state_conv[l], state_ffn[l], cache_mem_k[l], cache_mem_v[l])
        s_st.append(st)
    y_sample = x
    s_out = tuple(jnp.stack(f) for f in zip(*s_st))

    return (y_prompt, y_sample) + p_out + (jnp.stack(p_mk), jnp.stack(p_mv)) + s_out
```

```python
import functools

import jax
import jax.numpy as jnp
from jax import lax
from jax.experimental import pallas as pl
from jax.experimental.pallas import tpu as pltpu

F32 = jnp.float32
BF16 = jnp.bfloat16

D_MODEL = 1024
HEAD_DIM_A = 64
C_A = 256
H_A = 4
W_LORA = 64
A_LORA = 64
G_LORA = 128
RWKV_COLS = 3 * C_A + W_LORA + A_LORA + G_LORA
HEAD_DIM_B = 64
C_B = 512
H_B = 8
C_CONV = 256
CONV_W = 31
N_MEM = 256
H_X = 4
HEAD_DIM_X = 256
D_FF = 2816
FFN_CONV_W = 3
PAGE_SIZE = 128
RMS_EPS = 1e-6
LN_EPS = 1e-5
GN_EPS = 64e-5

COL_Q = RWKV_COLS
COL_K = COL_Q + C_B
COL_V = COL_K + C_B
COL_C = COL_V + C_B
COL_F = COL_C + 2 * C_CONV
LANE = 128
IN_COLS_PAD = COL_F + LANE

VMEM_LIMIT_BYTES = 56 * 1024 * 1024
NEG_BIG = -1e30


def _cparams(*sem):
    return pltpu.CompilerParams(dimension_semantics=sem, vmem_limit_bytes=VMEM_LIMIT_BYTES)


def _dot(a, b):
    return jnp.dot(a.astype(BF16), b.astype(BF16), preferred_element_type=F32)


def _dot_nt(a, b):
    return lax.dot_general(a.astype(BF16), b.astype(BF16), (((1,), (1,)), ((), ())),
                           preferred_element_type=F32)


def _dot_tn(a, b):
    return lax.dot_general(a.astype(BF16), b.astype(BF16), (((0,), (0,)), ((), ())),
                           preferred_element_type=F32)


def _split_bf16(x, n):
    parts, r = [], x
    for i in range(n):
        p = r.astype(BF16)
        parts.append(p)
        if i + 1 < n:
            r = r - p.astype(F32)
    return parts


def _dot_split_lhs(x, m_bf, n=3):
    acc = None
    for p in _split_bf16(x, n):
        d = jnp.dot(p, m_bf, preferred_element_type=F32)
        acc = d if acc is None else acc + d
    return acc


def _dot_split_rhs(m_bf, x, n=3):
    acc = None
    for p in _split_bf16(x, n):
        d = jnp.dot(m_bf, p, preferred_element_type=F32)
        acc = d if acc is None else acc + d
    return acc


def _dot_nt_split_rhs(m_bf, x, n=3):
    acc = None
    for p in _split_bf16(x, n):
        d = lax.dot_general(m_bf, p, (((1,), (1,)), ((), ())), preferred_element_type=F32)
        acc = d if acc is None else acc + d
    return acc


def _iota(shape, axis):
    return lax.broadcasted_iota(jnp.int32, shape, axis)


def _block_ones(n, seg):
    return (_iota((n, n), 0) // seg == _iota((n, n), 1) // seg).astype(BF16)


def _sigmoid(x):
    return 1.0 / (1.0 + jnp.exp(-x))


def _softplus(x):
    return jnp.maximum(x, 0.0) + jnp.log(1.0 + jnp.exp(-jnp.abs(x)))


def _rms(x, g):
    ms = jnp.mean(x * x, axis=-1, keepdims=True)
    return x * lax.rsqrt(ms + RMS_EPS) * g


def _norm_matmul_kernel(x_ref, g_ref, w_ref, o_ref):
    xn = _rms(x_ref[...], g_ref[...])
    o_ref[...] = jnp.dot(xn.astype(BF16), w_ref[...], preferred_element_type=F32)


def _norm_matmul(x2d, g, w_bf, tm):
    m, d = x2d.shape
    n = w_bf.shape[1]
    return pl.pallas_call(
        _norm_matmul_kernel,
        grid=(m // tm,),
        in_specs=[pl.BlockSpec((tm, d), lambda i: (i, 0)),
                  pl.BlockSpec((1, d), lambda i: (0, 0)),
                  pl.BlockSpec((d, n), lambda i: (0, 0))],
        out_specs=pl.BlockSpec((tm, n), lambda i: (i, 0)),
        out_shape=jax.ShapeDtypeStruct((m, n), F32),
        compiler_params=_cparams("parallel"),
        name="norm_matmul",
    )(x2d, g.reshape(1, d), w_bf)


def _mem_kv_kernel(m_ref, g_ref, wk_ref, wv_ref, kn_ref, k_out, v_out):
    xn = _rms(m_ref[...], g_ref[...]).astype(BF16)
    kraw = jnp.dot(xn, wk_ref[...], preferred_element_type=F32)
    v_out[...] = jnp.dot(xn, wv_ref[...], preferred_element_type=F32)
    for h in range(H_X):
        sl = slice(h * HEAD_DIM_X, (h + 1) * HEAD_DIM_X)
        k_out[:, sl] = _rms(kraw[:, sl], kn_ref[...])


def _mem_kv(mem2d, g, wk_bf, wv_bf, k_norm, tm):
    m, d = mem2d.shape
    row = lambda i: (i, 0)
    fix = lambda i: (0, 0)
    return pl.pallas_call(
        _mem_kv_kernel,
        grid=(m // tm,),
        in_specs=[pl.BlockSpec((tm, d), row), pl.BlockSpec((1, d), fix),
                  pl.BlockSpec((d, d), fix), pl.BlockSpec((d, d), fix),
                  pl.BlockSpec((1, HEAD_DIM_X), fix)],
        out_specs=[pl.BlockSpec((tm, d), row), pl.BlockSpec((tm, d), row)],
        out_shape=[jax.ShapeDtypeStruct((m, d), F32)] * 2,
        compiler_params=_cparams("parallel"),
        name="mem_kv",
    )(mem2d, g.reshape(1, d), wk_bf, wv_bf, k_norm.reshape(1, HEAD_DIM_X))


RWKV_CHUNK = 128
RWKV_UNROLL = 2

def _rwkv_kernel(z_ref, shift_ref, s0_ref, mu_ref, w0_ref, wup_ref, a0_ref, aup_ref, gup_ref,
                 kk_ref, ka_ref, rk_ref, lng_ref, lnb_ref,
                 y_ref, st_ref,
                 s_scr, zprev_scr, lw_scr, kn_scr, be_scr, km_scr, r_scr, v_scr, yacc_scr,
                 rhat_scr, gend_scr, glr_scr, dadd_scr,
                 *, tb, chunk, t_valid, unroll):
    ti = pl.program_id(1)

    @pl.when(ti == 0)
    def _():
        s_scr[...] = s0_ref[0]
        zprev_scr[...] = shift_ref[0]

    z = z_ref[0]
    row = _iota((tb, 1), 0)
    zprev = jnp.where(row == 0, zprev_scr[...], pltpu.roll(z, 1, axis=0))
    zprev_scr[...] = z[tb - 1:tb, :]
    zs = z + (zprev - z) * mu_ref[...]
    r = zs[:, 0:C_A]
    k = zs[:, C_A:2 * C_A]
    v = zs[:, 2 * C_A:3 * C_A]
    o_w = 3 * C_A
    wd = zs[:, o_w:o_w + W_LORA]
    ad = zs[:, o_w + W_LORA:o_w + W_LORA + A_LORA]
    gd = zs[:, o_w + W_LORA + A_LORA:RWKV_COLS]
    w_log = -_softplus(-(w0_ref[...] + _dot(jnp.tanh(wd), wup_ref[...]))) - 0.5
    lw = -jnp.exp(w_log)
    a = _sigmoid(a0_ref[...] + _dot(ad, aup_ref[...]))
    g = _dot(_sigmoid(gd), gup_ref[...])
    k_mod = k * (1.0 + (a - 1.0) * ka_ref[...])
    bd = _block_ones(C_A, HEAD_DIM_A)
    kk0 = k * kk_ref[...]
    kn = kk0 / jnp.maximum(jnp.sqrt(_dot_split_lhs(kk0 * kk0, bd)), 1e-12)
    if t_valid < tb:
        ok = row < t_valid
        lw = jnp.where(ok, lw, 0.0)
        kn = jnp.where(ok, kn, 0.0)
        k_mod = jnp.where(ok, k_mod, 0.0)
    lw_scr[...] = lw
    kn_scr[...] = kn
    be_scr[...] = kn * a
    km_scr[...] = k_mod
    r_scr[...] = r
    v_scr[...] = v

    c = chunk
    ri = _iota((c, c), 0)
    ci = _iota((c, c), 1)
    lincl = (ri >= ci).astype(BF16)
    m_strict = ri > ci
    m_incl = ri >= ci
    eye = (ri == ci).astype(F32)

    heads = range(H_A)
    hsl = [slice(h * HEAD_DIM_A, (h + 1) * HEAD_DIM_A) for h in heads]

    def chunk_terms(j, carry):
        rows = pl.ds(pl.multiple_of(j * c, c), c)
        lw_c = lw_scr[rows, :]
        cum = _dot_split_rhs(lincl, lw_c)
        g_inc = jnp.exp(cum)
        g_exc = jnp.exp(cum - lw_c)
        g_inv = jnp.exp(-cum)
        g_end = g_inc[c - 1:c, :]
        at = -kn_scr[rows, :] * g_exc
        bt = be_scr[rows, :] * g_inv
        kt = km_scr[rows, :] * g_inv
        rt = r_scr[rows, :] * g_inc
        vv = v_scr[rows, :]
        btc = bt * g_end
        ktc = kt * g_end
        a_ab = [jnp.where(m_strict, _dot_nt(at[:, s], bt[:, s]), 0.0) for s in hsl]
        a_ak = [jnp.where(m_strict, _dot_nt(at[:, s], kt[:, s]), 0.0) for s in hsl]
        m_rb = [jnp.where(m_incl, _dot_nt(rt[:, s], bt[:, s]), 0.0) for s in hsl]
        m_rk = [jnp.where(m_incl, _dot_nt(rt[:, s], kt[:, s]), 0.0) for s in hsl]
        x = [eye + a for a in a_ab]
        p = a_ab
        n = 1
        while 2 * n < c:
            p = [_dot(q, q) for q in p]
            x = [xi + _dot(pi, xi) for xi, pi in zip(x, p)]
            n *= 2
        a_hat = [_dot(x[h], at[:, hsl[h]]) for h in heads]
        u0 = [_dot(x[h], _dot(a_ak[h], vv[:, hsl[h]])) for h in heads]
        r_hat = [rt[:, hsl[h]] + _dot(m_rb[h], a_hat[h]) for h in heads]
        y0 = [_dot(m_rb[h], u0[h]) + _dot(m_rk[h], vv[:, hsl[h]]) for h in heads]
        g_lr = [_dot_tn(a_hat[h], btc[:, hsl[h]]) for h in heads]
        d_add = [_dot_tn(u0[h], btc[:, hsl[h]]) + _dot_tn(vv[:, hsl[h]], ktc[:, hsl[h]]) for h in heads]
        rhat_scr[rows, :] = jnp.concatenate(r_hat, axis=1)
        yacc_scr[rows, :] = jnp.concatenate(y0, axis=1)
        gend_scr[j] = jnp.broadcast_to(g_end, (8, C_A))
        for h in heads:
            glr_scr[j, h] = g_lr[h]
            dadd_scr[j, h] = d_add[h]
        return carry

    lax.fori_loop(0, tb // c, chunk_terms, 0, unroll=min(unroll, tb // c))

    def chunk_state(j, carry):
        rows = pl.ds(pl.multiple_of(j * c, c), c)
        s = [s_scr[h] for h in heads]
        r_hat = rhat_scr[rows, :]
        g_end = gend_scr[j][0:1, :]
        y = [_dot_nt(r_hat[:, hsl[h]], s[h]) for h in heads]
        s_new = [s[h] * g_end[:, hsl[h]] + _dot(s[h], glr_scr[j, h]) + dadd_scr[j, h] for h in heads]
        yacc_scr[rows, :] = yacc_scr[rows, :] + jnp.concatenate(y, axis=1)
        for h in heads:
            s_scr[h] = s_new[h]
        return carry

    lax.fori_loop(0, tb // c, chunk_state, 0)

    y = yacc_scr[...]
    inv_n = 1.0 / HEAD_DIM_A
    mean = _dot_split_lhs(y, bd) * inv_n
    yc = y - mean
    var = _dot_split_lhs(yc * yc, bd) * inv_n
    yn = yc * lax.rsqrt(var + GN_EPS) * lng_ref[...] + lnb_ref[...]
    bonus = _dot_split_lhs(r * k_mod * rk_ref[...], bd) * v
    y_ref[0] = ((yn + bonus) * g).astype(y_ref.dtype)

    @pl.when(ti == pl.num_programs(1) - 1)
    def _():
        st_ref[0] = s_scr[...]


def _rwkv(z3, shift0, s0, wl, *, tb, chunk, t_valid, unroll=1, out_dtype=BF16):
    b, t, _ = z3.shape
    vec = lambda n: pl.BlockSpec((1, n), lambda i, j: (0, 0))
    mat = lambda r, c_: pl.BlockSpec((r, c_), lambda i, j: (0, 0))
    kern = functools.partial(_rwkv_kernel, tb=tb, chunk=chunk, t_valid=t_valid, unroll=unroll)
    nck = tb // chunk
    scr = [pltpu.VMEM((H_A, HEAD_DIM_A, HEAD_DIM_A), F32), pltpu.VMEM((1, RWKV_COLS), F32)]
    scr += [pltpu.VMEM((tb, C_A), F32)] * 8
    scr += [pltpu.VMEM((nck, 8, C_A), F32)]
    scr += [pltpu.VMEM((nck, H_A, HEAD_DIM_A, HEAD_DIM_A), F32)] * 2
    return pl.pallas_call(
        kern,
        grid=(b, t // tb),
        in_specs=[pl.BlockSpec((1, tb, RWKV_COLS), lambda i, j: (i, j, 0)),
                  pl.BlockSpec((1, 1, RWKV_COLS), lambda i, j: (i, 0, 0)),
                  pl.BlockSpec((1, H_A, HEAD_DIM_A, HEAD_DIM_A), lambda i, j: (i, 0, 0, 0)),
                  vec(RWKV_COLS), vec(C_A), mat(W_LORA, C_A), vec(C_A), mat(A_LORA, C_A),
                  mat(G_LORA, C_A), vec(C_A), vec(C_A), vec(C_A), vec(C_A), vec(C_A)],
        out_specs=[pl.BlockSpec((1, tb, C_A), lambda i, j: (i, j, 0)),
                   pl.BlockSpec((1, H_A, HEAD_DIM_A, HEAD_DIM_A), lambda i, j: (i, 0, 0, 0))],
        out_shape=[jax.ShapeDtypeStruct((b, t, C_A), out_dtype),
                   jax.ShapeDtypeStruct((b, H_A, HEAD_DIM_A, HEAD_DIM_A), F32)],
        scratch_shapes=scr,
        compiler_params=_cparams("parallel", "arbitrary"),
        name="rwkv7",
    )(z3, shift0, s0, wl['rwkv_mu'].reshape(1, -1), wl['rwkv_w0'].reshape(1, -1), wl['rwkv_w_up_bf'],
      wl['rwkv_a0'].reshape(1, -1), wl['rwkv_a_up_bf'], wl['rwkv_g_up_bf'],
      wl['rwkv_k_k'].reshape(1, -1), wl['rwkv_k_a'].reshape(1, -1), wl['rwkv_r_k'].reshape(1, -1),
      wl['rwkv_ln_g'].reshape(1, -1), wl['rwkv_ln_b'].reshape(1, -1))


def _fox_norms(zq, zk, fl, qn_g, kn_g, bf):
    bd = _block_ones(C_B, HEAD_DIM_B)
    inv_n = 1.0 / HEAD_DIM_B
    q = zq * lax.rsqrt(_dot_split_lhs(zq * zq, bd, 2) * inv_n + RMS_EPS) * qn_g
    k = zk * lax.rsqrt(_dot_split_lhs(zk * zk, bd, 2) * inv_n + RMS_EPS) * kn_g
    logf = -_softplus(-(fl + bf))
    return q * (HEAD_DIM_B ** -0.5), k, logf


def _fox_norm_kernel(zq_ref, zk_ref, fl_ref, qn_ref, kn_ref, bf_ref, q_out, k_out, lf_out):
    q, k, logf = _fox_norms(zq_ref[...], zk_ref[...], fl_ref[...], qn_ref[...], kn_ref[...], bf_ref[...])
    q_out[...] = q
    k_out[...] = k
    lf_out[...] = logf


def _fox_norm(z2, qn_t, kn_t, bf_pad):
    m = z2.shape[0]
    fix = lambda i: (0, 0)
    return pl.pallas_call(
        _fox_norm_kernel,
        grid=(1,),
        in_specs=[pl.BlockSpec((m, C_B), lambda i: (0, COL_Q // C_B)),
                  pl.BlockSpec((m, C_B), lambda i: (0, COL_K // C_B)),
                  pl.BlockSpec((m, LANE), lambda i: (0, COL_F // LANE)),
                  pl.BlockSpec((1, C_B), fix), pl.BlockSpec((1, C_B), fix), pl.BlockSpec((1, LANE), fix)],
        out_specs=[pl.BlockSpec((m, C_B), fix), pl.BlockSpec((m, C_B), fix), pl.BlockSpec((m, LANE), fix)],
        out_shape=[jax.ShapeDtypeStruct((m, C_B), F32), jax.ShapeDtypeStruct((m, C_B), F32),
                   jax.ShapeDtypeStruct((m, LANE), F32)],
        compiler_params=_cparams("arbitrary"),
        name="fox_norm",
    )(z2, z2, z2, qn_t, kn_t, bf_pad)


N_AUG = 3
FOX_TILE = 512


def _fox_prep_kernel(zq_ref, zk_ref, zv_ref, fl_ref, qn_ref, kn_ref, bf_ref,
                     kt_out, vt_out, lft_out, qat_out, ka_out, vat_out, cum_scr, *, tp):
    ti = pl.program_id(1)

    @pl.when(ti == 0)
    def _():
        cum_scr[...] = jnp.zeros_like(cum_scr)

    q, k, logf = _fox_norms(zq_ref[0], zk_ref[0], fl_ref[0], qn_ref[...], kn_ref[...], bf_ref[...])
    qt = q.T
    kt = k.T
    vt = zv_ref[0].T
    kt_out[0] = kt
    vt_out[0] = vt
    lft_out[0] = logf.T[:H_B, :]
    lincl = (_iota((tp, tp), 0) >= _iota((tp, tp), 1)).astype(BF16)
    cum = _dot_split_rhs(lincl, logf) + cum_scr[...]
    cum_scr[...] = cum[tp - 1:tp, :]
    cumt = cum.T
    lane = _iota((tp, LANE), 1)
    sub = _iota((8, tp), 0)
    ones_rows = jnp.ones((HEAD_DIM_B, tp), F32)
    zero_rows = jnp.zeros((LANE - HEAD_DIM_B - 8, tp), F32)
    for h in range(H_B):
        hs = slice(h * HEAD_DIM_B, (h + 1) * HEAD_DIM_B)
        c_row = jnp.broadcast_to(cumt[h:h + 1, :], (8, tp))
        aug_q = jnp.where((sub >= N_AUG) & (sub < 2 * N_AUG), 1.0, 0.0)
        for i, p in enumerate(_split_bf16(c_row, N_AUG)):
            aug_q = jnp.where(sub == i, p.astype(F32), aug_q)
        qat_out[0, h, 0] = jnp.concatenate([qt[hs, :], aug_q, zero_rows], axis=0).astype(BF16)
        vat_out[0, h, 0] = jnp.concatenate([vt[hs, :], ones_rows], axis=0).astype(BF16)
        c_col = jnp.broadcast_to(cum[:, h:h + 1], (tp, LANE))
        aug_k = jnp.where((lane >= HEAD_DIM_B) & (lane < HEAD_DIM_B + N_AUG), 1.0, 0.0)
        for i, p in enumerate(_split_bf16(c_col, N_AUG)):
            aug_k = jnp.where(lane == HEAD_DIM_B + N_AUG + i, -p.astype(F32), aug_k)
        k2 = k[:, (h // 2) * LANE:(h // 2 + 1) * LANE]
        if h % 2 == 1:
            k2 = pltpu.roll(k2, HEAD_DIM_B, axis=1)
        ka_out[0, h, 0] = jnp.where(lane < HEAD_DIM_B, k2, aug_k).astype(BF16)


def _fox_prep(z3, qn_t, kn_t, bf_pad, tp):
    b, t, _ = z3.shape
    nt = t // tp
    fix = lambda i, j: (0, 0)
    kern = functools.partial(_fox_prep_kernel, tp=tp)
    tr_spec = pl.BlockSpec((1, H_B, 1, LANE, tp), lambda i, j: (i, 0, j, 0, 0))
    return pl.pallas_call(
        kern,
        grid=(b, nt),
        in_specs=[pl.BlockSpec((1, tp, C_B), lambda i, j: (i, j, COL_Q // C_B)),
                  pl.BlockSpec((1, tp, C_B), lambda i, j: (i, j, COL_K // C_B)),
                  pl.BlockSpec((1, tp, C_B), lambda i, j: (i, j, COL_V // C_B)),
                  pl.BlockSpec((1, tp, LANE), lambda i, j: (i, j, COL_F // LANE)),
                  pl.BlockSpec((1, C_B), fix), pl.BlockSpec((1, C_B), fix), pl.BlockSpec((1, LANE), fix)],
        out_specs=[pl.BlockSpec((1, C_B, tp), lambda i, j: (i, 0, j)),
                   pl.BlockSpec((1, C_B, tp), lambda i, j: (i, 0, j)),
                   pl.BlockSpec((1, H_B, tp), lambda i, j: (i, 0, j)),
                   tr_spec,
                   pl.BlockSpec((1, H_B, 1, tp, LANE), lambda i, j: (i, 0, j, 0, 0)),
                   tr_spec],
        out_shape=[jax.ShapeDtypeStruct((b, C_B, t), F32),
                   jax.ShapeDtypeStruct((b, C_B, t), F32),
                   jax.ShapeDtypeStruct((b, H_B, t), F32),
                   jax.ShapeDtypeStruct((b, H_B, nt, LANE, tp), BF16),
                   jax.ShapeDtypeStruct((b, H_B, nt, tp, LANE), BF16),
                   jax.ShapeDtypeStruct((b, H_B, nt, LANE, tp), BF16)],
        scratch_shapes=[pltpu.VMEM((1, LANE), F32)],
        compiler_params=_cparams("parallel", "arbitrary"),
        name="fox_prep",
    )(z3, z3, z3, z3, qn_t, kn_t, bf_pad)


def _fox_attn_kernel(qt_ref, k_ref, vt_ref, o_ref, m0_scr, m1_scr, acc0_scr, acc1_scr, *, tq):
    qi = pl.program_id(2)
    m_scrs = (m0_scr, m1_scr)
    acc_scrs = (acc0_scr, acc1_scr)
    for hh in range(2):
        m_scrs[hh][...] = jnp.full_like(m_scrs[hh], NEG_BIG)
        acc_scrs[hh][...] = jnp.zeros_like(acc_scrs[hh])
    two = range(2)

    def block(kj, diagonal):
        st = [jnp.dot(k_ref[0, hh, kj], qt_ref[0, hh, 0], preferred_element_type=F32) for hh in two]
        if diagonal:
            keep = _iota((tq, tq), 0) <= _iota((tq, tq), 1)
            st = [jnp.where(keep, s, -jnp.inf) for s in st]
        m_old = [m_scrs[hh][...] for hh in two]
        m_new = [jnp.maximum(m_old[hh], jnp.max(st[hh], axis=0, keepdims=True)) for hh in two]
        alpha = [jnp.exp(m_old[hh] - m_new[hh]) for hh in two]
        p = [jnp.exp(st[hh] - m_new[hh]).astype(BF16) for hh in two]
        pv = [jnp.dot(vt_ref[0, hh, kj], p[hh], preferred_element_type=F32) for hh in two]
        for hh in two:
            acc_scrs[hh][...] = alpha[hh] * acc_scrs[hh][...] + pv[hh]
            m_scrs[hh][...] = m_new[hh]

    def body(kj, carry):
        block(kj, False)
        return carry

    lax.fori_loop(0, qi, body, 0)
    block(qi, True)
    ot = [acc_scrs[hh][0:HEAD_DIM_B, :] / acc_scrs[hh][HEAD_DIM_B:HEAD_DIM_B + 1, :] for hh in two]
    o_ref[0] = jnp.concatenate(ot, axis=0).T.astype(o_ref.dtype)


def _fox_attn(qat, ka, vat):
    b, _, nt, _, tq = qat.shape
    kern = functools.partial(_fox_attn_kernel, tq=tq)
    return pl.pallas_call(
        kern,
        grid=(b, H_B // 2, nt),
        in_specs=[pl.BlockSpec((1, 2, 1, LANE, tq), lambda i, h, j: (i, h, j, 0, 0)),
                  pl.BlockSpec((1, 2, nt, tq, LANE), lambda i, h, j: (i, h, 0, 0, 0)),
                  pl.BlockSpec((1, 2, nt, LANE, tq), lambda i, h, j: (i, h, 0, 0, 0))],
        out_specs=pl.BlockSpec((1, tq, LANE), lambda i, h, j: (i, j, h)),
        out_shape=jax.ShapeDtypeStruct((b, nt * tq, C_B), BF16),
        scratch_shapes=[pltpu.VMEM((1, tq), F32), pltpu.VMEM((1, tq), F32),
                        pltpu.VMEM((LANE, tq), F32), pltpu.VMEM((LANE, tq), F32)],
        compiler_params=_cparams("parallel", "parallel", "arbitrary"),
        name="fox_attn",
    )(qat, ka, vat)


PAGES_PER_STEP = 8
N_ROWS = 32
N_GATE_SPLIT = 3


def _fox_paged_kernel(pt_ref, q_ref, kn_ref, vn_ref, lfn_ref, *rest, n_q, n_steps):
    pp = PAGES_PER_STEP
    k_refs, v_refs, lf_refs = rest[0:pp], rest[pp:2 * pp], rest[2 * pp:3 * pp]
    o_ref = rest[3 * pp]
    qbd_scr, m_scr, l_scr, acc_scr, car_scr, cq_scr = rest[3 * pp + 1:]
    j = pl.program_id(1)
    p_sz = PAGE_SIZE
    r_head = _iota((N_ROWS, 1), 0) % H_B
    r_q = _iota((N_ROWS, 1), 0) // H_B
    ki = _iota((p_sz, p_sz), 0)
    kj = _iota((p_sz, p_sz), 1)
    per_query = lambda a: jnp.concatenate([a] * n_q, axis=0)

    def update(s_pages, vt_pages):
        s = jnp.concatenate(s_pages, axis=1)
        m_old = m_scr[...]
        m_new = jnp.maximum(m_old, jnp.max(s, axis=1, keepdims=True))
        alpha = jnp.exp(m_old - m_new)
        p = jnp.exp(s - m_new)
        l_scr[...] = alpha * l_scr[...] + jnp.sum(p, axis=1, keepdims=True)
        m_scr[...] = m_new
        pv = None
        for i, vt in enumerate(vt_pages):
            d = _dot_nt(p[:, i * p_sz:(i + 1) * p_sz], vt)
            pv = d if pv is None else pv + d
        acc_scr[...] = alpha * acc_scr[...] + pv

    @pl.when(j == 0)
    def _():
        lane_head = _iota((N_ROWS, C_B), 1) // HEAD_DIM_B
        qbd = jnp.where(lane_head == r_head, q_ref[0], 0.0).astype(BF16)
        qbd_scr[...] = qbd
        m_scr[...] = jnp.full_like(m_scr, NEG_BIG)
        l_scr[...] = jnp.zeros_like(l_scr)
        acc_scr[...] = jnp.zeros_like(acc_scr)
        car_scr[...] = jnp.zeros_like(car_scr)
        lft = per_query(lfn_ref[0])
        ct = _dot_split_lhs(lft, (ki <= kj).astype(BF16))
        key = _iota((N_ROWS, p_sz), 1)
        cq = jnp.sum(jnp.where(key == r_q, ct, 0.0), axis=1, keepdims=True)
        cq_scr[...] = cq
        s = _dot(qbd, kn_ref[0]) + cq - ct
        s = jnp.where(key <= r_q, s, -jnp.inf)
        update([s], [vn_ref[0]])

    pieces = []
    for i in range(pp):
        pieces += [p.astype(F32) for p in _split_bf16(lf_refs[i][...], N_GATE_SPLIT)]
    suffix_and_total = jnp.concatenate([(ki > kj).astype(BF16), jnp.ones((p_sz, p_sz), BF16)], axis=1)
    res = jnp.dot(jnp.concatenate(pieces, axis=0).astype(BF16), suffix_and_total,
                  preferred_element_type=F32)
    qbd = qbd_scr[...]
    cq = cq_scr[...]
    after = car_scr[...]
    s_pages, vt_pages = [], []
    for i in range(pp):
        r0 = i * N_GATE_SPLIT * H_B
        both = res[r0:r0 + H_B, :]
        for n in range(1, N_GATE_SPLIT):
            both = both + res[r0 + n * H_B:r0 + (n + 1) * H_B, :]
        ex = both[:, :p_sz] + after
        after = after + both[:, p_sz:]
        s_pages.append(_dot(qbd, k_refs[i][...].reshape(C_B, p_sz)) + cq + per_query(ex))
        vt_pages.append(v_refs[i][...].reshape(C_B, p_sz))
    car_scr[...] = after
    update(s_pages, vt_pages)

    @pl.when(j == n_steps - 1)
    def _():
        lane_head = _iota((N_ROWS, C_B), 1) // HEAD_DIM_B
        o = jnp.where(lane_head == r_head, acc_scr[...] / l_scr[...], 0.0)
        o_ref[0] = jnp.sum(o.reshape(n_q, H_B, C_B), axis=1)


def _fox_paged(page_table, q32, kn_pad, vn_pad, lfn_pad, cache_k, cache_v, cache_lf, layer, n_q):
    db, n_pages = page_table.shape
    pp = PAGES_PER_STEP
    n_steps = n_pages // pp
    kern = functools.partial(_fox_paged_kernel, n_q=n_q, n_steps=n_steps)

    def kv_map(i):
        return lambda b, j, pt: (layer, pt[b, n_pages - 1 - (j * pp + i)], 0, 0, 0)

    def lf_map(i):
        return lambda b, j, pt: (layer, pt[b, n_pages - 1 - (j * pp + i)], 0, 0)

    per_b = lambda b, j, pt: (b, 0, 0)
    kv_specs = [pl.BlockSpec((None, None, H_B, HEAD_DIM_B, PAGE_SIZE), kv_map(i)) for i in range(pp)]
    lf_specs = [pl.BlockSpec((None, None, H_B, PAGE_SIZE), lf_map(i)) for i in range(pp)]
    grid_spec = pltpu.PrefetchScalarGridSpec(
        num_scalar_prefetch=1,
        grid=(db, n_steps),
        in_specs=[pl.BlockSpec((1, N_ROWS, C_B), per_b),
                  pl.BlockSpec((1, C_B, PAGE_SIZE), per_b),
                  pl.BlockSpec((1, C_B, PAGE_SIZE), per_b),
                  pl.BlockSpec((1, H_B, PAGE_SIZE), per_b)] + kv_specs + kv_specs + lf_specs,
        out_specs=pl.BlockSpec((1, n_q, C_B), per_b),
        scratch_shapes=[pltpu.VMEM((N_ROWS, C_B), BF16), pltpu.VMEM((N_ROWS, 1), F32),
                        pltpu.VMEM((N_ROWS, 1), F32), pltpu.VMEM((N_ROWS, C_B), F32),
                        pltpu.VMEM((H_B, PAGE_SIZE), F32), pltpu.VMEM((N_ROWS, 1), F32)],
    )
    return pl.pallas_call(
        kern,
        grid_spec=grid_spec,
        out_shape=jax.ShapeDtypeStruct((db, n_q, C_B), F32),
        compiler_params=_cparams("parallel", "arbitrary"),
        name="fox_paged",
    )(page_table, q32, kn_pad, vn_pad, lfn_pad, *([cache_k] * pp), *([cache_v] * pp), *([cache_lf] * pp))


CONV_PAD = 32
CONV_ROWS = 256


def _conv_kernel(z_ref, buf_ref, w_ref, b_ref, g_ref, beta_ref, y_ref, nb_ref, xp_scr, *, t):
    z = z_ref[0]
    u = z[:, :C_CONV] * _sigmoid(z[:, C_CONV:])
    off = CONV_PAD - (CONV_W - 1)
    xp_scr[off:CONV_PAD, :] = buf_ref[0]
    xp_scr[CONV_PAD:CONV_PAD + t, :] = u
    nb_ref[0] = xp_scr[t + off:t + CONV_PAD, :]
    rt = min(t, CONV_ROWS)
    for r0 in range(0, t, rt):
        acc = jnp.zeros((rt, C_CONV), F32) + b_ref[...]
        for j in range(CONV_W):
            acc = acc + w_ref[j:j + 1, :] * xp_scr[r0 + off + j:r0 + off + j + rt, :]
        mean = jnp.mean(acc, axis=-1, keepdims=True)
        xc = acc - mean
        var = jnp.mean(xc * xc, axis=-1, keepdims=True)
        yn = xc * lax.rsqrt(var + LN_EPS) * g_ref[...] + beta_ref[...]
        y_ref[0, r0:r0 + rt, :] = (yn * _sigmoid(yn)).astype(y_ref.dtype)


def _conv_module(z3, buf, w, bias, ln_g, ln_b, out_dtype=BF16):
    b, t, _ = z3.shape
    fix = lambda i: (0, 0)
    kern = functools.partial(_conv_kernel, t=t)
    return pl.pallas_call(
        kern,
        grid=(b,),
        in_specs=[pl.BlockSpec((1, t, 2 * C_CONV), lambda i: (i, 0, COL_C // (2 * C_CONV))),
                  pl.BlockSpec((1, CONV_W - 1, C_CONV), lambda i: (i, 0, 0)),
                  pl.BlockSpec((CONV_W, C_CONV), fix), pl.BlockSpec((1, C_CONV), fix),
                  pl.BlockSpec((1, C_CONV), fix), pl.BlockSpec((1, C_CONV), fix)],
        out_specs=[pl.BlockSpec((1, t, C_CONV), lambda i: (i, 0, 0)),
                   pl.BlockSpec((1, CONV_W - 1, C_CONV), lambda i: (i, 0, 0))],
        out_shape=[jax.ShapeDtypeStruct((b, t, C_CONV), out_dtype),
                   jax.ShapeDtypeStruct((b, CONV_W - 1, C_CONV), F32)],
        scratch_shapes=[pltpu.VMEM((t + CONV_PAD, C_CONV), F32)],
        compiler_params=_cparams("parallel"),
        name="conv_module",
    )(z3, buf, w, bias.reshape(1, -1), ln_g.reshape(1, -1), ln_b.reshape(1, -1))


def _pre_attn_kernel(x_ref, ya_ref, yb_ref, yc_ref, woa_ref, wob_ref, woc_ref, gx_ref, wq_ref, qn_ref,
                     x1_ref, q_ref):
    x1 = (x_ref[...] + jnp.dot(ya_ref[...], woa_ref[...], preferred_element_type=F32)
          + jnp.dot(yb_ref[...], wob_ref[...], preferred_element_type=F32)
          + jnp.dot(yc_ref[...], woc_ref[...], preferred_element_type=F32))
    x1_ref[...] = x1
    q = jnp.dot(_rms(x1, gx_ref[...]).astype(BF16), wq_ref[...], preferred_element_type=F32)
    for h in range(H_X):
        sl = slice(h * HEAD_DIM_X, (h + 1) * HEAD_DIM_X)
        q_ref[:, sl] = (_rms(q[:, sl], qn_ref[...]) * (HEAD_DIM_X ** -0.5)).astype(q_ref.dtype)


def _pre_attn(x2d, ya, yb, yc, wl, tm):
    m, d = x2d.shape
    row = lambda i: (i, 0)
    fix = lambda i: (0, 0)
    return pl.pallas_call(
        _pre_attn_kernel,
        grid=(m // tm,),
        in_specs=[pl.BlockSpec((tm, d), row), pl.BlockSpec((tm, C_A), row), pl.BlockSpec((tm, C_B), row),
                  pl.BlockSpec((tm, C_CONV), row),
                  pl.BlockSpec((C_A, d), fix), pl.BlockSpec((C_B, d), fix), pl.BlockSpec((C_CONV, d), fix),
                  pl.BlockSpec((1, d), fix), pl.BlockSpec((d, d), fix), pl.BlockSpec((1, HEAD_DIM_X), fix)],
        out_specs=[pl.BlockSpec((tm, d), row), pl.BlockSpec((tm, d), row)],
        out_shape=[jax.ShapeDtypeStruct((m, d), F32), jax.ShapeDtypeStruct((m, d), BF16)],
        compiler_params=_cparams("parallel"),
        name="pre_attn",
    )(x2d, ya, yb, yc, wl['w_out_a'], wl['w_out_b'], wl['w_out_c'], wl['g_x'].reshape(1, d),
      wl['w_xq_bf'], wl['xq_norm'].reshape(1, HEAD_DIM_X))


MEM_LANE_CHUNKS = HEAD_DIM_X // LANE
MEM_ROWS_PER_TOKEN = H_X * MEM_LANE_CHUNKS


def _xattn_kernel(q_ref, k_ref, v_ref, o_ref, *, row_major):
    q = q_ref[0]

    def head_of(ref, h):
        if row_major:
            return ref[0, :, h * HEAD_DIM_X:(h + 1) * HEAD_DIM_X].astype(BF16)
        parts = [ref[0, pl.ds(c * H_X + h, N_MEM, stride=MEM_ROWS_PER_TOKEN), :]
                 for c in range(MEM_LANE_CHUNKS)]
        return jnp.concatenate(parts, axis=1).astype(BF16)

    for h in range(H_X):
        sl = slice(h * HEAD_DIM_X, (h + 1) * HEAD_DIM_X)
        s = lax.dot_general(q[:, sl], head_of(k_ref, h), (((1,), (1,)), ((), ())),
                            preferred_element_type=F32)
        m = jnp.max(s, axis=1, keepdims=True)
        p = jnp.exp(s - m)
        p = p / jnp.sum(p, axis=1, keepdims=True)
        o_ref[0, :, sl] = jnp.dot(p.astype(BF16), head_of(v_ref, h),
                                  preferred_element_type=F32).astype(o_ref.dtype)


def _xattn(q3, mk, mv, tq, row_major=True, layer=None):
    b, t, d = q3.shape
    if layer is None:
        kv_spec = pl.BlockSpec((1,) + mk.shape[1:], lambda i, j: (i, 0, 0))
    else:
        kv_spec = pl.BlockSpec((None, 1) + mk.shape[2:], lambda i, j: (layer, i, 0, 0))
    return pl.pallas_call(
        functools.partial(_xattn_kernel, row_major=row_major),
        grid=(b, t // tq),
        in_specs=[pl.BlockSpec((1, tq, d), lambda i, j: (i, j, 0)), kv_spec, kv_spec],
        out_specs=pl.BlockSpec((1, tq, d), lambda i, j: (i, j, 0)),
        out_shape=jax.ShapeDtypeStruct((b, t, d), BF16),
        compiler_params=_cparams("parallel", "parallel"),
        name="xattn",
    )(q3, mk, mv)


def _mem_device_rows(a):
    n_l, b = a.shape[:2]
    a = a.reshape(n_l, b, N_MEM, H_X, MEM_LANE_CHUNKS, LANE)
    return jnp.transpose(a, (0, 1, 2, 4, 3, 5)).reshape(n_l, b, N_MEM * MEM_ROWS_PER_TOKEN, LANE)


FFN_TN = 256
FFN_CHUNKS = D_FF // FFN_TN


def _ffn_kernel(*refs, tm, tiles_per_seq, seg, carried):
    if carried:
        (x1_ref, o_ref, wo_ref, g_ref, wg_ref, wv_ref, cwg_ref, cwv_ref, cbg_ref, cbv_ref, wd_ref,
         y_ref, hg_ref, hv_ref, x2_scr, xn_scr, acc_scr, cg_scr, cv_scr) = refs
    else:
        (x1_ref, o_ref, wo_ref, g_ref, wg_ref, wv_ref, cwg_ref, cwv_ref, cbg_ref, cbv_ref, wd_ref,
         p1g_ref, p1v_ref, p2g_ref, p2v_ref,
         y_ref, hg_ref, hv_ref, x2_scr, xn_scr, acc_scr) = refs
    mi = pl.program_id(0)
    c = pl.program_id(1)

    @pl.when(c == 0)
    def _():
        x2 = x1_ref[...] + jnp.dot(o_ref[...], wo_ref[...], preferred_element_type=F32)
        x2_scr[...] = x2
        xn_scr[...] = _rms(x2, g_ref[...]).astype(BF16)
        acc_scr[...] = jnp.zeros_like(acc_scr)

    xn = xn_scr[...]
    row = _iota((tm, 1), 0)
    halves = []
    for name in ("g", "v"):
        w_ref, cw_ref, cb_ref = (wg_ref, cwg_ref, cbg_ref) if name == "g" else (wv_ref, cwv_ref, cbv_ref)
        h = jnp.dot(xn, w_ref[...], preferred_element_type=F32)
        hm1 = pltpu.roll(h, 1, axis=0)
        hm2 = pltpu.roll(h, 2, axis=0)
        if carried:
            c_scr = cg_scr if name == "g" else cv_scr
            prev = jnp.where(mi % tiles_per_seq == 0, 0.0, c_scr[c])
            hm1 = jnp.where(row == 0, prev[7:8, :], hm1)
            hm2 = jnp.where(row == 0, prev[6:7, :], jnp.where(row == 1, prev[7:8, :], hm2))
            c_scr[c] = h[tm - 8:tm, :]
            out_ref = hg_ref if name == "g" else hv_ref
            out_ref[0] = h[tm - 8:tm, :]
        else:
            p1 = (p1g_ref if name == "g" else p1v_ref)[...]
            p2 = (p2g_ref if name == "g" else p2v_ref)[...]
            hm1 = jnp.where(row % seg == 0, p1, hm1)
            hm2 = jnp.where(row % seg < 2, p2, hm2)
            out_ref = hg_ref if name == "g" else hv_ref
            out_ref[...] = h
        halves.append(cw_ref[0:1, :] * hm2 + cw_ref[1:2, :] * hm1 + cw_ref[2:3, :] * h + cb_ref[...])
    gate, val = halves
    act = gate * _sigmoid(gate) * val
    acc_scr[...] += jnp.dot(act.astype(BF16), wd_ref[...], preferred_element_type=F32)

    @pl.when(c == FFN_CHUNKS - 1)
    def _():
        y_ref[...] = x2_scr[...] + acc_scr[...]


def _ffn(x1, o, wl, *, tm, tiles_per_seq=1, seg=1, prev=None):
    m, d = x1.shape
    carried = prev is None
    tn = FFN_TN
    row = lambda i, c: (i, 0)
    fix = lambda i, c: (0, 0)
    gcol = lambda i, c: (0, c)
    vcol = lambda i, c: (0, c + FFN_CHUNKS)
    in_specs = [pl.BlockSpec((tm, d), row), pl.BlockSpec((tm, d), row), pl.BlockSpec((d, d), fix),
                pl.BlockSpec((1, d), fix),
                pl.BlockSpec((d, tn), gcol), pl.BlockSpec((d, tn), vcol),
                pl.BlockSpec((FFN_CONV_W, tn), gcol), pl.BlockSpec((FFN_CONV_W, tn), vcol),
                pl.BlockSpec((1, tn), gcol), pl.BlockSpec((1, tn), vcol),
                pl.BlockSpec((tn, d), lambda i, c: (c, 0))]
    args = [x1, o, wl['w_xo_bf'], wl['g_ffn'].reshape(1, d), wl['w_up_bf'], wl['w_up_bf'],
            wl['ffn_conv_w'], wl['ffn_conv_w'], wl['ffn_conv_b'].reshape(1, -1),
            wl['ffn_conv_b'].reshape(1, -1), wl['w_down_bf']]
    scratch = [pltpu.VMEM((tm, d), F32), pltpu.VMEM((tm, d), BF16), pltpu.VMEM((tm, d), F32)]
    if carried:
        h_shape = jax.ShapeDtypeStruct((m // tm, 8, D_FF), F32)
        h_spec = pl.BlockSpec((1, 8, tn), lambda i, c: (i, 0, c))
        scratch += [pltpu.VMEM((FFN_CHUNKS, 8, tn), F32)] * 2
    else:
        p1, p2 = prev
        in_specs += [pl.BlockSpec((tm, tn), lambda i, c: (i, c)),
                     pl.BlockSpec((tm, tn), lambda i, c: (i, c + FFN_CHUNKS))] * 2
        args += [p1, p1, p2, p2]
        h_shape = jax.ShapeDtypeStruct((m, D_FF), F32)
        h_spec = pl.BlockSpec((tm, tn), lambda i, c: (i, c))
    kern = functools.partial(_ffn_kernel, tm=tm, tiles_per_seq=tiles_per_seq, seg=seg, carried=carried)
    return pl.pallas_call(
        kern,
        grid=(m // tm, FFN_CHUNKS),
        in_specs=in_specs,
        out_specs=[pl.BlockSpec((tm, d), row), h_spec, h_spec],
        out_shape=[jax.ShapeDtypeStruct((m, d), F32), h_shape, h_shape],
        scratch_shapes=scratch,
        compiler_params=_cparams("arbitrary", "arbitrary"),
        name="ffn",
    )(*args)


def _prep_weights(w, l):
    wl = {n: a[l] for n, a in w.items()}
    w_in = wl['w_in']
    o_b = RWKV_COLS
    o_c = RWKV_COLS + 3 * C_B + H_B
    pad = jnp.zeros((D_MODEL, LANE - H_B), F32)
    w_in2 = jnp.concatenate([w_in[:, :o_b], w_in[:, o_b:o_b + 3 * C_B], w_in[:, o_c:],
                             w_in[:, o_b + 3 * C_B:o_c], pad], axis=1)
    wl['w_in_bf'] = w_in2.astype(BF16)
    for n in ('rwkv_w_up', 'rwkv_a_up', 'rwkv_g_up', 'w_xq', 'w_xk', 'w_xv', 'w_xo', 'w_up', 'w_down'):
        wl[n + '_bf'] = wl[n].astype(BF16)
    w_out = wl['w_out'].astype(BF16)
    wl['w_out_a'] = w_out[:C_A]
    wl['w_out_b'] = w_out[C_A:C_A + C_B]
    wl['w_out_c'] = w_out[C_A + C_B:]
    wl['fox_q_norm_t'] = jnp.tile(wl['fox_q_norm'], H_B).reshape(1, C_B)
    wl['fox_k_norm_t'] = jnp.tile(wl['fox_k_norm'], H_B).reshape(1, C_B)
    wl['fox_b_f_pad'] = jnp.concatenate([wl['fox_b_f'], jnp.zeros((LANE - H_B,), F32)]).reshape(1, LANE)
    return wl


def _pick(n, prefs):
    for p in prefs:
        if n % p == 0:
            return p
    return n


def _prompt_layer(x, mem2d, wl):
    b, t, d = x.shape
    m = b * t
    tm = _pick(m, (512, 256, 128, 64, 32, 16, 8))
    mk, mv = _mem_kv(mem2d, wl['g_mem'], wl['w_xk_bf'], wl['w_xv_bf'], wl['xk_norm'],
                     _pick(mem2d.shape[0], (512, 256)))
    z2 = _norm_matmul(x.reshape(m, d), wl['g_mix'], wl['w_in_bf'], tm)
    z3 = z2.reshape(b, t, IN_COLS_PAD)
    tb = _pick(t, (512, 256, 128, 64))
    ya, s_t = _rwkv(z3, jnp.zeros((b, 1, RWKV_COLS), F32), jnp.zeros((b, H_A, HEAD_DIM_A, HEAD_DIM_A), F32),
                    wl, tb=tb, chunk=min(RWKV_CHUNK, tb), t_valid=tb, unroll=RWKV_UNROLL)
    tp = _pick(t, (FOX_TILE, 256, 128))
    kt, vt, lft, qat, ka, vat = _fox_prep(z3, wl['fox_q_norm_t'], wl['fox_k_norm_t'], wl['fox_b_f_pad'], tp)
    yb = _fox_attn(qat, ka, vat)
    yc, conv_buf = _conv_module(z3, jnp.zeros((b, CONV_W - 1, C_CONV), F32), wl['conv_w'], wl['conv_b'],
                                wl['conv_ln_g'], wl['conv_ln_b'])
    x1, q = _pre_attn(x.reshape(m, d), ya.reshape(m, C_A), yb.reshape(m, C_B), yc.reshape(m, C_CONV), wl, tm)
    o = _xattn(q.reshape(b, t, d), mk.reshape(b, N_MEM, d), mv.reshape(b, N_MEM, d),
               _pick(t, (512, 256, 128, 64)))
    tf = _pick(t, (1024, 512, 256, 128, 64))
    y, hg, hv = _ffn(x1, o.reshape(m, d), wl, tm=tf, tiles_per_seq=t // tf)
    tiles = t // tf
    ffn_buf = jnp.concatenate([hg.reshape(b, tiles, 8, D_FF)[:, -1, 6:], hv.reshape(b, tiles, 8, D_FF)[:, -1, 6:]],
                              axis=-1)
    st = (jnp.transpose(kt.reshape(b, H_B, HEAD_DIM_B, t), (0, 3, 1, 2)),
          jnp.transpose(vt.reshape(b, H_B, HEAD_DIM_B, t), (0, 3, 1, 2)),
          jnp.transpose(lft, (0, 2, 1)),
          s_t,
          z3[:, t - 1:t, :RWKV_COLS],
          conv_buf,
          ffn_buf)
    return y.reshape(b, t, d), st, mk.reshape(b, N_MEM, H_X, HEAD_DIM_X), mv.reshape(b, N_MEM, H_X, HEAD_DIM_X)


def _sample_layer(x, wl, layer, cache_k, cache_v, cache_lf, page_table, s0, shift0, conv0, ffn0, mk, mv):
    b, t, d = x.shape
    m = b * t
    z2 = _norm_matmul(x.reshape(m, d), wl['g_mix'], wl['w_in_bf'], m)
    z3 = z2.reshape(b, t, IN_COLS_PAD)
    t_pad = 8
    za = jnp.pad(z3[:, :, :RWKV_COLS], ((0, 0), (0, t_pad - t), (0, 0)))
    ya, s_t = _rwkv(za, shift0, s0, wl, tb=t_pad, chunk=t_pad, t_valid=t, out_dtype=F32)
    ya = ya[:, :t].astype(BF16)
    q, k_n, lf = _fox_norm(z2, wl['fox_q_norm_t'], wl['fox_k_norm_t'], wl['fox_b_f_pad'])
    zv = z3[:, :, COL_V:COL_V + C_B]
    page_t = lambda a: jnp.pad(jnp.swapaxes(a, 1, 2), ((0, 0), (0, 0), (0, PAGE_SIZE - t)))
    q32 = jnp.repeat(q.reshape(b, t, C_B), H_B, axis=1)
    yb = _fox_paged(page_table, q32, page_t(k_n.reshape(b, t, C_B)), page_t(zv),
                    page_t(lf.reshape(b, t, LANE)[:, :, :H_B]), cache_k, cache_v, cache_lf, layer, t)
    yb = yb.astype(BF16)
    yc, conv_buf = _conv_module(z3, conv0, wl['conv_w'], wl['conv_b'], wl['conv_ln_g'], wl['conv_ln_b'], F32)
    yc = yc.astype(BF16)
    x1, q = _pre_attn(x.reshape(m, d), ya.reshape(m, C_A), yb.reshape(m, C_B), yc.reshape(m, C_CONV), wl, m)
    t16 = 16
    qp = jnp.pad(q.reshape(b, t, d), ((0, 0), (0, t16 - t), (0, 0)))
    o = _xattn(qp, mk, mv, t16, row_major=False, layer=layer)[:, :t]
    zrow = jnp.zeros((b, 1, 2 * D_FF), F32)
    p1 = jnp.concatenate([ffn0[:, 1:2], jnp.tile(zrow, (1, t - 1, 1))], axis=1).reshape(m, 2 * D_FF)
    p2 = jnp.concatenate([ffn0, jnp.tile(zrow, (1, t - 2, 1))], axis=1).reshape(m, 2 * D_FF)
    y, hg, hv = _ffn(x1, o.reshape(m, d), wl, tm=m, seg=t, prev=(p1, p2))
    ffn_buf = jnp.concatenate([hg.reshape(b, t, D_FF)[:, t - 2:], hv.reshape(b, t, D_FF)[:, t - 2:]], axis=-1)
    st = (k_n.reshape(b, t, H_B, HEAD_DIM_B), zv.reshape(b, t, H_B, HEAD_DIM_B),
          lf.reshape(b, t, LANE)[:, :, :H_B], s_t, z3[:, t - 1:t, :RWKV_COLS], conv_buf, ffn_buf)
    return y.reshape(b, t, d), st


def kernel(x_prompt, x_sample, mem_prompt, cache_fox_k, cache_fox_v, cache_fox_logf, page_table, state_rwkv, state_rwkv_shift, state_conv, state_ffn, cache_mem_k, cache_mem_v, g_mix, w_in, rwkv_mu, rwkv_w0, rwkv_w_up, rwkv_a0, rwkv_a_up, rwkv_g_up, rwkv_k_k, rwkv_k_a, rwkv_r_k, rwkv_ln_g, rwkv_ln_b, fox_q_norm, fox_k_norm, fox_b_f, conv_w, conv_b, conv_ln_g, conv_ln_b, w_out, g_x, g_mem, w_xq, w_xk, w_xv, xq_norm, xk_norm, w_xo, g_ffn, w_up, ffn_conv_w, ffn_conv_b, w_down):
    w = dict(g_mix=g_mix, w_in=w_in, rwkv_mu=rwkv_mu, rwkv_w0=rwkv_w0, rwkv_w_up=rwkv_w_up, rwkv_a0=rwkv_a0,
             rwkv_a_up=rwkv_a_up, rwkv_g_up=rwkv_g_up, rwkv_k_k=rwkv_k_k, rwkv_k_a=rwkv_k_a, rwkv_r_k=rwkv_r_k,
             rwkv_ln_g=rwkv_ln_g, rwkv_ln_b=rwkv_ln_b, fox_q_norm=fox_q_norm, fox_k_norm=fox_k_norm,
             fox_b_f=fox_b_f, conv_w=conv_w, conv_b=conv_b, conv_ln_g=conv_ln_g, conv_ln_b=conv_ln_b,
             w_out=w_out, g_x=g_x, g_mem=g_mem, w_xq=w_xq, w_xk=w_xk, w_xv=w_xv, xq_norm=xq_norm,
             xk_norm=xk_norm, w_xo=w_xo, g_ffn=g_ffn, w_up=w_up, ffn_conv_w=ffn_conv_w,
             ffn_conv_b=ffn_conv_b, w_down=w_down)
    depth = w_in.shape[0]
    wls = [_prep_weights(w, l) for l in range(depth)]

    b, n_mem, d = mem_prompt.shape
    mem2d = mem_prompt.reshape(b * n_mem, d)
    x = x_prompt
    p_st, p_mk, p_mv = [], [], []
    for l in range(depth):
        x, st, mk, mv = _prompt_layer(x, mem2d, wls[l])
        p_st.append(st)
        p_mk.append(mk)
        p_mv.append(mv)
    y_prompt = x
    p_out = tuple(jnp.stack(f) for f in zip(*p_st))

    ck = jnp.transpose(cache_fox_k, (0, 1, 3, 4, 2))
    cv = jnp.transpose(cache_fox_v, (0, 1, 3, 4, 2))
    clf = jnp.transpose(cache_fox_logf, (0, 1, 3, 2))
    mem_k_rows = _mem_device_rows(cache_mem_k)
    mem_v_rows = _mem_device_rows(cache_mem_v)
    x = x_sample
    s_st = []
    for l in range(depth):
        x, st = _sample_layer(x, wls[l], l, ck, cv, clf, page_table, state_rwkv[l],
                              state_rwkv_shift[l], state_conv[l], state_ffn[l], mem_k_rows, mem_v_rows)
        s_st.append(st)
    y_sample = x
    s_out = tuple(jnp.stack(f) for f in zip(*s_st))

    return (y_prompt, y_sample) + p_out + (jnp.stack(p_mk), jnp.stack(p_mv)) + s_out
```

```python
import functools

import jax
import jax.numpy as jnp
from jax import lax
from jax.experimental import pallas as pl
from jax.experimental.pallas import tpu as pltpu

F32 = jnp.float32
BF16 = jnp.bfloat16

D_MODEL = 1024
HEAD_DIM_A = 64
C_A = 256
H_A = 4
W_LORA = 64
A_LORA = 64
G_LORA = 128
RWKV_COLS = 3 * C_A + W_LORA + A_LORA + G_LORA
HEAD_DIM_B = 64
C_B = 512
H_B = 8
C_CONV = 256
CONV_W = 31
N_MEM = 256
H_X = 4
HEAD_DIM_X = 256
D_FF = 2816
FFN_CONV_W = 3
PAGE_SIZE = 128
RMS_EPS = 1e-6
LN_EPS = 1e-5
GN_EPS = 64e-5

COL_Q = RWKV_COLS
COL_K = COL_Q + C_B
COL_V = COL_K + C_B
COL_C = COL_V + C_B
COL_F = COL_C + 2 * C_CONV
LANE = 128
IN_COLS_PAD = COL_F + LANE

VMEM_LIMIT_BYTES = 56 * 1024 * 1024
NEG_BIG = -1e30


def _cparams(*sem):
    return pltpu.CompilerParams(dimension_semantics=sem, vmem_limit_bytes=VMEM_LIMIT_BYTES)


def _dot(a, b):
    return jnp.dot(a.astype(BF16), b.astype(BF16), preferred_element_type=F32)


def _dot_nt(a, b):
    return lax.dot_general(a.astype(BF16), b.astype(BF16), (((1,), (1,)), ((), ())),
                           preferred_element_type=F32)


def _dot_tn(a, b):
    return lax.dot_general(a.astype(BF16), b.astype(BF16), (((0,), (0,)), ((), ())),
                           preferred_element_type=F32)


def _split_bf16(x, n):
    parts, r = [], x
    for i in range(n):
        p = r.astype(BF16)
        parts.append(p)
        if i + 1 < n:
            r = r - p.astype(F32)
    return parts


def _dot_split_lhs(x, m_bf, n=3):
    acc = None
    for p in _split_bf16(x, n):
        d = jnp.dot(p, m_bf, preferred_element_type=F32)
        acc = d if acc is None else acc + d
    return acc


def _dot_split_rhs(m_bf, x, n=3):
    acc = None
    for p in _split_bf16(x, n):
        d = jnp.dot(m_bf, p, preferred_element_type=F32)
        acc = d if acc is None else acc + d
    return acc


def _dot_nt_split_rhs(m_bf, x, n=3):
    acc = None
    for p in _split_bf16(x, n):
        d = lax.dot_general(m_bf, p, (((1,), (1,)), ((), ())), preferred_element_type=F32)
        acc = d if acc is None else acc + d
    return acc


def _iota(shape, axis):
    return lax.broadcasted_iota(jnp.int32, shape, axis)


def _block_ones(n, seg):
    return (_iota((n, n), 0) // seg == _iota((n, n), 1) // seg).astype(BF16)


def _sigmoid(x):
    return 1.0 / (1.0 + jnp.exp(-x))


def _softplus(x):
    return jnp.maximum(x, 0.0) + jnp.log(1.0 + jnp.exp(-jnp.abs(x)))


def _rms(x, g):
    ms = jnp.mean(x * x, axis=-1, keepdims=True)
    return x * lax.rsqrt(ms + RMS_EPS) * g


def _norm_matmul_kernel(x_ref, g_ref, w_ref, o_ref):
    xn = _rms(x_ref[...], g_ref[...])
    o_ref[...] = jnp.dot(xn.astype(BF16), w_ref[...], preferred_element_type=F32)


def _norm_matmul(x2d, g, w_bf, tm):
    m, d = x2d.shape
    n = w_bf.shape[1]
    return pl.pallas_call(
        _norm_matmul_kernel,
        grid=(m // tm,),
        in_specs=[pl.BlockSpec((tm, d), lambda i: (i, 0)),
                  pl.BlockSpec((1, d), lambda i: (0, 0)),
                  pl.BlockSpec((d, n), lambda i: (0, 0))],
        out_specs=pl.BlockSpec((tm, n), lambda i: (i, 0)),
        out_shape=jax.ShapeDtypeStruct((m, n), F32),
        compiler_params=_cparams("parallel"),
        name="norm_matmul",
    )(x2d, g.reshape(1, d), w_bf)


def _mem_kv_kernel(m_ref, g_ref, wk_ref, wv_ref, kn_ref, k_out, v_out):
    xn = _rms(m_ref[...], g_ref[...]).astype(BF16)
    kraw = jnp.dot(xn, wk_ref[...], preferred_element_type=F32)
    v_out[...] = jnp.dot(xn, wv_ref[...], preferred_element_type=F32)
    for h in range(H_X):
        sl = slice(h * HEAD_DIM_X, (h + 1) * HEAD_DIM_X)
        k_out[:, sl] = _rms(kraw[:, sl], kn_ref[...])


def _mem_kv(mem2d, g, wk_bf, wv_bf, k_norm, tm):
    m, d = mem2d.shape
    row = lambda i: (i, 0)
    fix = lambda i: (0, 0)
    return pl.pallas_call(
        _mem_kv_kernel,
        grid=(m // tm,),
        in_specs=[pl.BlockSpec((tm, d), row), pl.BlockSpec((1, d), fix),
                  pl.BlockSpec((d, d), fix), pl.BlockSpec((d, d), fix),
                  pl.BlockSpec((1, HEAD_DIM_X), fix)],
        out_specs=[pl.BlockSpec((tm, d), row), pl.BlockSpec((tm, d), row)],
        out_shape=[jax.ShapeDtypeStruct((m, d), F32)] * 2,
        compiler_params=_cparams("parallel"),
        name="mem_kv",
    )(mem2d, g.reshape(1, d), wk_bf, wv_bf, k_norm.reshape(1, HEAD_DIM_X))


RWKV_CHUNK = 128
RWKV_UNROLL = 2

def _rwkv_kernel(z_ref, shift_ref, s0_ref, mu_ref, w0_ref, wup_ref, a0_ref, aup_ref, gup_ref,
                 kk_ref, ka_ref, rk_ref, lng_ref, lnb_ref,
                 y_ref, st_ref,
                 s_scr, zprev_scr, lw_scr, kn_scr, be_scr, km_scr, r_scr, v_scr, yacc_scr,
                 rhat_scr, gend_scr, glr_scr, dadd_scr,
                 *, tb, chunk, t_valid, unroll, seqs):
    ti = pl.program_id(1)
    row = _iota((tb, 1), 0)
    if seqs == 1:
        @pl.when(ti == 0)
        def _():
            s_scr[...] = s0_ref[0]
            zprev_scr[...] = shift_ref[0]

        z = z_ref[0]
        pos = row
        zprev = jnp.where(row == 0, zprev_scr[...], pltpu.roll(z, 1, axis=0))
        zprev_scr[...] = z[tb - 1:tb, :]
    else:
        z = z_ref[...].reshape(tb, RWKV_COLS)
        pos = row % chunk
        before = jnp.broadcast_to(shift_ref[...], (seqs, chunk, RWKV_COLS)).reshape(tb, RWKV_COLS)
        zprev = jnp.where(pos == 0, before, pltpu.roll(z, 1, axis=0))
    zs = z + (zprev - z) * mu_ref[...]
    r = zs[:, 0:C_A]
    k = zs[:, C_A:2 * C_A]
    v = zs[:, 2 * C_A:3 * C_A]
    o_w = 3 * C_A
    wd = zs[:, o_w:o_w + W_LORA]
    ad = zs[:, o_w + W_LORA:o_w + W_LORA + A_LORA]
    gd = zs[:, o_w + W_LORA + A_LORA:RWKV_COLS]
    w_log = -_softplus(-(w0_ref[...] + _dot(jnp.tanh(wd), wup_ref[...]))) - 0.5
    lw = -jnp.exp(w_log)
    a = _sigmoid(a0_ref[...] + _dot(ad, aup_ref[...]))
    g = _dot(_sigmoid(gd), gup_ref[...])
    k_mod = k * (1.0 + (a - 1.0) * ka_ref[...])
    bd = _block_ones(C_A, HEAD_DIM_A)
    kk0 = k * kk_ref[...]
    kn = kk0 / jnp.maximum(jnp.sqrt(_dot_split_lhs(kk0 * kk0, bd)), 1e-12)
    if t_valid < tb // seqs:
        ok = pos < t_valid
        lw = jnp.where(ok, lw, 0.0)
        kn = jnp.where(ok, kn, 0.0)
        k_mod = jnp.where(ok, k_mod, 0.0)
    lw_scr[...] = lw
    kn_scr[...] = kn
    be_scr[...] = kn * a
    km_scr[...] = k_mod
    r_scr[...] = r
    v_scr[...] = v

    c = chunk
    ri = _iota((c, c), 0)
    ci = _iota((c, c), 1)
    lincl = (ri >= ci).astype(BF16)
    m_strict = ri > ci
    m_incl = ri >= ci
    eye = (ri == ci).astype(F32)

    heads = range(H_A)
    hsl = [slice(h * HEAD_DIM_A, (h + 1) * HEAD_DIM_A) for h in heads]

    def chunk_terms(j, carry):
        rows = pl.ds(pl.multiple_of(j * c, c), c)
        lw_c = lw_scr[rows, :]
        cum = _dot_split_rhs(lincl, lw_c)
        g_inc = jnp.exp(cum)
        g_exc = jnp.exp(cum - lw_c)
        g_inv = jnp.exp(-cum)
        g_end = g_inc[c - 1:c, :]
        at = -kn_scr[rows, :] * g_exc
        bt = be_scr[rows, :] * g_inv
        kt = km_scr[rows, :] * g_inv
        rt = r_scr[rows, :] * g_inc
        vv = v_scr[rows, :]
        btc = bt * g_end
        ktc = kt * g_end
        a_ab = [jnp.where(m_strict, _dot_nt(at[:, s], bt[:, s]), 0.0) for s in hsl]
        a_ak = [jnp.where(m_strict, _dot_nt(at[:, s], kt[:, s]), 0.0) for s in hsl]
        m_rb = [jnp.where(m_incl, _dot_nt(rt[:, s], bt[:, s]), 0.0) for s in hsl]
        m_rk = [jnp.where(m_incl, _dot_nt(rt[:, s], kt[:, s]), 0.0) for s in hsl]
        x = [eye + a for a in a_ab]
        p = a_ab
        n = 1
        while 2 * n < c:
            p = [_dot(q, q) for q in p]
            x = [xi + _dot(pi, xi) for xi, pi in zip(x, p)]
            n *= 2
        a_hat = [_dot(x[h], at[:, hsl[h]]) for h in heads]
        u0 = [_dot(x[h], _dot(a_ak[h], vv[:, hsl[h]])) for h in heads]
        r_hat = [rt[:, hsl[h]] + _dot(m_rb[h], a_hat[h]) for h in heads]
        y0 = [_dot(m_rb[h], u0[h]) + _dot(m_rk[h], vv[:, hsl[h]]) for h in heads]
        g_lr = [_dot_tn(a_hat[h], btc[:, hsl[h]]) for h in heads]
        d_add = [_dot_tn(u0[h], btc[:, hsl[h]]) + _dot_tn(vv[:, hsl[h]], ktc[:, hsl[h]]) for h in heads]
        rhat_scr[rows, :] = jnp.concatenate(r_hat, axis=1)
        yacc_scr[rows, :] = jnp.concatenate(y0, axis=1)
        gend_scr[j] = jnp.broadcast_to(g_end, (8, C_A))
        for h in heads:
            glr_scr[j, h] = g_lr[h]
            dadd_scr[j, h] = d_add[h]
        return carry

    lax.fori_loop(0, tb // c, chunk_terms, 0, unroll=min(unroll, tb // c))

    def chunk_state(j, carry):
        rows = pl.ds(pl.multiple_of(j * c, c), c)
        s = [s_scr[h] if seqs == 1 else s0_ref[j, h] for h in heads]
        r_hat = rhat_scr[rows, :]
        g_end = gend_scr[j][0:1, :]
        y = [_dot_nt(r_hat[:, hsl[h]], s[h]) for h in heads]
        s_new = [s[h] * g_end[:, hsl[h]] + _dot(s[h], glr_scr[j, h]) + dadd_scr[j, h] for h in heads]
        yacc_scr[rows, :] = yacc_scr[rows, :] + jnp.concatenate(y, axis=1)
        for h in heads:
            if seqs == 1:
                s_scr[h] = s_new[h]
            else:
                st_ref[j, h] = s_new[h]
        return carry

    lax.fori_loop(0, tb // c, chunk_state, 0)

    y = yacc_scr[...]
    inv_n = 1.0 / HEAD_DIM_A
    mean = _dot_split_lhs(y, bd) * inv_n
    yc = y - mean
    var = _dot_split_lhs(yc * yc, bd) * inv_n
    yn = yc * lax.rsqrt(var + GN_EPS) * lng_ref[...] + lnb_ref[...]
    bonus = _dot_split_lhs(r * k_mod * rk_ref[...], bd) * v
    out = ((yn + bonus) * g).astype(y_ref.dtype)
    if seqs == 1:
        y_ref[0] = out

        @pl.when(ti == pl.num_programs(1) - 1)
        def _():
            st_ref[0] = s_scr[...]
    else:
        y_ref[...] = out.reshape(seqs, chunk, C_A)


def _rwkv(z3, shift0, s0, wl, *, tb, chunk, t_valid, unroll=1, seqs=1, out_dtype=BF16):
    b, t, _ = z3.shape
    vec = lambda n: pl.BlockSpec((1, n), lambda i, j: (0, 0))
    mat = lambda r, c_: pl.BlockSpec((r, c_), lambda i, j: (0, 0))
    kern = functools.partial(_rwkv_kernel, tb=tb, chunk=chunk, t_valid=t_valid, unroll=unroll, seqs=seqs)
    nck = tb // chunk
    scr = [pltpu.VMEM((H_A, HEAD_DIM_A, HEAD_DIM_A), F32), pltpu.VMEM((1, RWKV_COLS), F32)]
    scr += [pltpu.VMEM((tb, C_A), F32)] * 8
    scr += [pltpu.VMEM((nck, 8, C_A), F32)]
    scr += [pltpu.VMEM((nck, H_A, HEAD_DIM_A, HEAD_DIM_A), F32)] * 2
    t_blk = tb // seqs
    state_spec = pl.BlockSpec((seqs, H_A, HEAD_DIM_A, HEAD_DIM_A), lambda i, j: (i, 0, 0, 0))
    return pl.pallas_call(
        kern,
        grid=(b // seqs, t // t_blk),
        in_specs=[pl.BlockSpec((seqs, t_blk, RWKV_COLS), lambda i, j: (i, j, 0)),
                  pl.BlockSpec((seqs, 1, RWKV_COLS), lambda i, j: (i, 0, 0)),
                  state_spec,
                  vec(RWKV_COLS), vec(C_A), mat(W_LORA, C_A), vec(C_A), mat(A_LORA, C_A),
                  mat(G_LORA, C_A), vec(C_A), vec(C_A), vec(C_A), vec(C_A), vec(C_A)],
        out_specs=[pl.BlockSpec((seqs, t_blk, C_A), lambda i, j: (i, j, 0)), state_spec],
        out_shape=[jax.ShapeDtypeStruct((b, t, C_A), out_dtype),
                   jax.ShapeDtypeStruct((b, H_A, HEAD_DIM_A, HEAD_DIM_A), F32)],
        scratch_shapes=scr,
        compiler_params=_cparams("parallel", "arbitrary"),
        name="rwkv7",
    )(z3, shift0, s0, wl['rwkv_mu'].reshape(1, -1), wl['rwkv_w0'].reshape(1, -1), wl['rwkv_w_up_bf'],
      wl['rwkv_a0'].reshape(1, -1), wl['rwkv_a_up_bf'], wl['rwkv_g_up_bf'],
      wl['rwkv_k_k'].reshape(1, -1), wl['rwkv_k_a'].reshape(1, -1), wl['rwkv_r_k'].reshape(1, -1),
      wl['rwkv_ln_g'].reshape(1, -1), wl['rwkv_ln_b'].reshape(1, -1))


def _fox_norms(zq, zk, fl, qn_g, kn_g, bf):
    bd = _block_ones(C_B, HEAD_DIM_B)
    inv_n = 1.0 / HEAD_DIM_B
    q = zq * lax.rsqrt(_dot_split_lhs(zq * zq, bd, 2) * inv_n + RMS_EPS) * qn_g
    k = zk * lax.rsqrt(_dot_split_lhs(zk * zk, bd, 2) * inv_n + RMS_EPS) * kn_g
    logf = -_softplus(-(fl + bf))
    return q * (HEAD_DIM_B ** -0.5), k, logf


def _fox_norm_kernel(zq_ref, zk_ref, fl_ref, qn_ref, kn_ref, bf_ref, q_out, k_out, lf_out):
    q, k, logf = _fox_norms(zq_ref[...], zk_ref[...], fl_ref[...], qn_ref[...], kn_ref[...], bf_ref[...])
    q_out[...] = q
    k_out[...] = k
    lf_out[...] = logf


def _fox_norm(z2, qn_t, kn_t, bf_pad):
    m = z2.shape[0]
    fix = lambda i: (0, 0)
    return pl.pallas_call(
        _fox_norm_kernel,
        grid=(1,),
        in_specs=[pl.BlockSpec((m, C_B), lambda i: (0, COL_Q // C_B)),
                  pl.BlockSpec((m, C_B), lambda i: (0, COL_K // C_B)),
                  pl.BlockSpec((m, LANE), lambda i: (0, COL_F // LANE)),
                  pl.BlockSpec((1, C_B), fix), pl.BlockSpec((1, C_B), fix), pl.BlockSpec((1, LANE), fix)],
        out_specs=[pl.BlockSpec((m, C_B), fix), pl.BlockSpec((m, C_B), fix), pl.BlockSpec((m, LANE), fix)],
        out_shape=[jax.ShapeDtypeStruct((m, C_B), F32), jax.ShapeDtypeStruct((m, C_B), F32),
                   jax.ShapeDtypeStruct((m, LANE), F32)],
        compiler_params=_cparams("arbitrary"),
        name="fox_norm",
    )(z2, z2, z2, qn_t, kn_t, bf_pad)


N_AUG = 3
FOX_TILE = 512


def _fox_prep_kernel(zq_ref, zk_ref, zv_ref, fl_ref, qn_ref, kn_ref, bf_ref,
                     kt_out, vt_out, lft_out, qat_out, ka_out, vat_out, cum_scr, *, tp):
    ti = pl.program_id(1)

    @pl.when(ti == 0)
    def _():
        cum_scr[...] = jnp.zeros_like(cum_scr)

    q, k, logf = _fox_norms(zq_ref[0], zk_ref[0], fl_ref[0], qn_ref[...], kn_ref[...], bf_ref[...])
    qt = q.T
    kt = k.T
    vt = zv_ref[0].T
    kt_out[0] = kt
    vt_out[0] = vt
    lft_out[0] = logf.T[:H_B, :]
    lincl = (_iota((tp, tp), 0) >= _iota((tp, tp), 1)).astype(BF16)
    cum = _dot_split_rhs(lincl, logf) + cum_scr[...]
    cum_scr[...] = cum[tp - 1:tp, :]
    cumt = cum.T
    lane = _iota((tp, LANE), 1)
    sub = _iota((8, tp), 0)
    ones_rows = jnp.ones((HEAD_DIM_B, tp), F32)
    zero_rows = jnp.zeros((LANE - HEAD_DIM_B - 8, tp), F32)
    for h in range(H_B):
        hs = slice(h * HEAD_DIM_B, (h + 1) * HEAD_DIM_B)
        c_row = jnp.broadcast_to(cumt[h:h + 1, :], (8, tp))
        aug_q = jnp.where((sub >= N_AUG) & (sub < 2 * N_AUG), 1.0, 0.0)
        for i, p in enumerate(_split_bf16(c_row, N_AUG)):
            aug_q = jnp.where(sub == i, p.astype(F32), aug_q)
        qat_out[0, h, 0] = jnp.concatenate([qt[hs, :], aug_q, zero_rows], axis=0).astype(BF16)
        vat_out[0, h, 0] = jnp.concatenate([vt[hs, :], ones_rows], axis=0).astype(BF16)
        c_col = jnp.broadcast_to(cum[:, h:h + 1], (tp, LANE))
        aug_k = jnp.where((lane >= HEAD_DIM_B) & (lane < HEAD_DIM_B + N_AUG), 1.0, 0.0)
        for i, p in enumerate(_split_bf16(c_col, N_AUG)):
            aug_k = jnp.where(lane == HEAD_DIM_B + N_AUG + i, -p.astype(F32), aug_k)
        k2 = k[:, (h // 2) * LANE:(h // 2 + 1) * LANE]
        if h % 2 == 1:
            k2 = pltpu.roll(k2, HEAD_DIM_B, axis=1)
        ka_out[0, h, 0] = jnp.where(lane < HEAD_DIM_B, k2, aug_k).astype(BF16)


def _fox_prep(z3, qn_t, kn_t, bf_pad, tp):
    b, t, _ = z3.shape
    nt = t // tp
    fix = lambda i, j: (0, 0)
    kern = functools.partial(_fox_prep_kernel, tp=tp)
    tr_spec = pl.BlockSpec((1, H_B, 1, LANE, tp), lambda i, j: (i, 0, j, 0, 0))
    return pl.pallas_call(
        kern,
        grid=(b, nt),
        in_specs=[pl.BlockSpec((1, tp, C_B), lambda i, j: (i, j, COL_Q // C_B)),
                  pl.BlockSpec((1, tp, C_B), lambda i, j: (i, j, COL_K // C_B)),
                  pl.BlockSpec((1, tp, C_B), lambda i, j: (i, j, COL_V // C_B)),
                  pl.BlockSpec((1, tp, LANE), lambda i, j: (i, j, COL_F // LANE)),
                  pl.BlockSpec((1, C_B), fix), pl.BlockSpec((1, C_B), fix), pl.BlockSpec((1, LANE), fix)],
        out_specs=[pl.BlockSpec((1, C_B, tp), lambda i, j: (i, 0, j)),
                   pl.BlockSpec((1, C_B, tp), lambda i, j: (i, 0, j)),
                   pl.BlockSpec((1, H_B, tp), lambda i, j: (i, 0, j)),
                   tr_spec,
                   pl.BlockSpec((1, H_B, 1, tp, LANE), lambda i, j: (i, 0, j, 0, 0)),
                   tr_spec],
        out_shape=[jax.ShapeDtypeStruct((b, C_B, t), F32),
                   jax.ShapeDtypeStruct((b, C_B, t), F32),
                   jax.ShapeDtypeStruct((b, H_B, t), F32),
                   jax.ShapeDtypeStruct((b, H_B, nt, LANE, tp), BF16),
                   jax.ShapeDtypeStruct((b, H_B, nt, tp, LANE), BF16),
                   jax.ShapeDtypeStruct((b, H_B, nt, LANE, tp), BF16)],
        scratch_shapes=[pltpu.VMEM((1, LANE), F32)],
        compiler_params=_cparams("parallel", "arbitrary"),
        name="fox_prep",
    )(z3, z3, z3, z3, qn_t, kn_t, bf_pad)


def _fox_attn_kernel(qt_ref, k_ref, vt_ref, o_ref, m0_scr, m1_scr, acc0_scr, acc1_scr, *, tq):
    qi = pl.program_id(2)
    m_scrs = (m0_scr, m1_scr)
    acc_scrs = (acc0_scr, acc1_scr)
    for hh in range(2):
        m_scrs[hh][...] = jnp.full_like(m_scrs[hh], NEG_BIG)
        acc_scrs[hh][...] = jnp.zeros_like(acc_scrs[hh])
    two = range(2)

    def block(kj, diagonal):
        st = [jnp.dot(k_ref[0, hh, kj], qt_ref[0, hh, 0], preferred_element_type=F32) for hh in two]
        if diagonal:
            keep = _iota((tq, tq), 0) <= _iota((tq, tq), 1)
            st = [jnp.where(keep, s, -jnp.inf) for s in st]
        m_old = [m_scrs[hh][...] for hh in two]
        m_new = [jnp.maximum(m_old[hh], jnp.max(st[hh], axis=0, keepdims=True)) for hh in two]
        alpha = [jnp.exp(m_old[hh] - m_new[hh]) for hh in two]
        p = [jnp.exp(st[hh] - m_new[hh]).astype(BF16) for hh in two]
        pv = [jnp.dot(vt_ref[0, hh, kj], p[hh], preferred_element_type=F32) for hh in two]
        for hh in two:
            acc_scrs[hh][...] = alpha[hh] * acc_scrs[hh][...] + pv[hh]
            m_scrs[hh][...] = m_new[hh]

    def body(kj, carry):
        block(kj, False)
        return carry

    lax.fori_loop(0, qi, body, 0)
    block(qi, True)
    ot = [acc_scrs[hh][0:HEAD_DIM_B, :] / acc_scrs[hh][HEAD_DIM_B:HEAD_DIM_B + 1, :] for hh in two]
    o_ref[0] = jnp.concatenate(ot, axis=0).T.astype(o_ref.dtype)


def _fox_attn(qat, ka, vat):
    b, _, nt, _, tq = qat.shape
    kern = functools.partial(_fox_attn_kernel, tq=tq)
    return pl.pallas_call(
        kern,
        grid=(b, H_B // 2, nt),
        in_specs=[pl.BlockSpec((1, 2, 1, LANE, tq), lambda i, h, j: (i, h, j, 0, 0)),
                  pl.BlockSpec((1, 2, nt, tq, LANE), lambda i, h, j: (i, h, 0, 0, 0)),
                  pl.BlockSpec((1, 2, nt, LANE, tq), lambda i, h, j: (i, h, 0, 0, 0))],
        out_specs=pl.BlockSpec((1, tq, LANE), lambda i, h, j: (i, j, h)),
        out_shape=jax.ShapeDtypeStruct((b, nt * tq, C_B), BF16),
        scratch_shapes=[pltpu.VMEM((1, tq), F32), pltpu.VMEM((1, tq), F32),
                        pltpu.VMEM((LANE, tq), F32), pltpu.VMEM((LANE, tq), F32)],
        compiler_params=_cparams("parallel", "parallel", "arbitrary"),
        name="fox_attn",
    )(qat, ka, vat)


PAGES_PER_STEP = 8
N_ROWS = 32
N_GATE_SPLIT = 3


def _fox_paged_kernel(pt_ref, q_ref, kn_ref, vn_ref, lfn_ref, lf_pool_ref, *rest, n_q, n_steps, n_pages):
    pp = PAGES_PER_STEP
    k_refs, v_refs = rest[0:pp], rest[pp:2 * pp]
    o_ref = rest[2 * pp]
    qbd_scr, m_scr, l_scr, acc_scr, car_scr, cq_scr = rest[2 * pp + 1:]
    b = pl.program_id(0)
    j = pl.program_id(1)
    p_sz = PAGE_SIZE
    r_head = _iota((N_ROWS, 1), 0) % H_B
    r_q = _iota((N_ROWS, 1), 0) // H_B
    ki = _iota((p_sz, p_sz), 0)
    kj = _iota((p_sz, p_sz), 1)
    per_query = lambda a: jnp.concatenate([a] * n_q, axis=0)

    def update(s_pages, vt_pages):
        s = jnp.concatenate(s_pages, axis=1)
        m_old = m_scr[...]
        m_new = jnp.maximum(m_old, jnp.max(s, axis=1, keepdims=True))
        alpha = jnp.exp(m_old - m_new)
        p = jnp.exp(s - m_new)
        l_scr[...] = alpha * l_scr[...] + jnp.sum(p, axis=1, keepdims=True)
        m_scr[...] = m_new
        pv = None
        for i, vt in enumerate(vt_pages):
            d = _dot_nt(p[:, i * p_sz:(i + 1) * p_sz], vt)
            pv = d if pv is None else pv + d
        acc_scr[...] = alpha * acc_scr[...] + pv

    @pl.when(j == 0)
    def _():
        lane_head = _iota((N_ROWS, C_B), 1) // HEAD_DIM_B
        qbd = jnp.where(lane_head == r_head, q_ref[0], 0.0).astype(BF16)
        qbd_scr[...] = qbd
        m_scr[...] = jnp.full_like(m_scr, NEG_BIG)
        l_scr[...] = jnp.zeros_like(l_scr)
        acc_scr[...] = jnp.zeros_like(acc_scr)
        car_scr[...] = jnp.zeros_like(car_scr)
        lft = per_query(lfn_ref[0])
        ct = _dot_split_lhs(lft, (ki <= kj).astype(BF16))
        key = _iota((N_ROWS, p_sz), 1)
        cq = jnp.sum(jnp.where(key == r_q, ct, 0.0), axis=1, keepdims=True)
        cq_scr[...] = cq
        s = _dot(qbd, kn_ref[0]) + cq - ct
        s = jnp.where(key <= r_q, s, -jnp.inf)
        update([s], [vn_ref[0]])

    pieces = []
    for i in range(pp):
        lf_page = lf_pool_ref[pt_ref[b, n_pages - 1 - (j * pp + i)]]
        pieces += [p.astype(F32) for p in _split_bf16(lf_page, N_GATE_SPLIT)]
    suffix_and_total = jnp.concatenate([(ki > kj).astype(BF16), jnp.ones((p_sz, p_sz), BF16)], axis=1)
    res = jnp.dot(jnp.concatenate(pieces, axis=0).astype(BF16), suffix_and_total,
                  preferred_element_type=F32)
    qbd = qbd_scr[...]
    cq = cq_scr[...]
    after = car_scr[...]
    s_pages, vt_pages = [], []
    for i in range(pp):
        r0 = i * N_GATE_SPLIT * H_B
        both = res[r0:r0 + H_B, :]
        for n in range(1, N_GATE_SPLIT):
            both = both + res[r0 + n * H_B:r0 + (n + 1) * H_B, :]
        ex = both[:, :p_sz] + after
        after = after + both[:, p_sz:]
        s_pages.append(_dot(qbd, k_refs[i][...].reshape(C_B, p_sz)) + cq + per_query(ex))
        vt_pages.append(v_refs[i][...].reshape(C_B, p_sz))
    car_scr[...] = after
    update(s_pages, vt_pages)

    @pl.when(j == n_steps - 1)
    def _():
        lane_head = _iota((N_ROWS, C_B), 1) // HEAD_DIM_B
        o = jnp.where(lane_head == r_head, acc_scr[...] / l_scr[...], 0.0)
        o_ref[0] = jnp.sum(o.reshape(n_q, H_B, C_B), axis=1)


def _fox_paged(page_table, q32, kn_pad, vn_pad, lfn_pad, cache_k, cache_v, cache_lf, layer, n_q):
    db, n_pages = page_table.shape
    pp = PAGES_PER_STEP
    n_steps = n_pages // pp
    kern = functools.partial(_fox_paged_kernel, n_q=n_q, n_steps=n_steps, n_pages=n_pages)
    n_phys = cache_lf.shape[1]

    def kv_map(i):
        return lambda b, j, pt: (layer, pt[b, n_pages - 1 - (j * pp + i)], 0, 0, 0)

    per_b = lambda b, j, pt: (b, 0, 0)
    kv_specs = [pl.BlockSpec((None, None, H_B, HEAD_DIM_B, PAGE_SIZE), kv_map(i)) for i in range(pp)]
    grid_spec = pltpu.PrefetchScalarGridSpec(
        num_scalar_prefetch=1,
        grid=(db, n_steps),
        in_specs=[pl.BlockSpec((1, N_ROWS, C_B), per_b),
                  pl.BlockSpec((1, C_B, PAGE_SIZE), per_b),
                  pl.BlockSpec((1, C_B, PAGE_SIZE), per_b),
                  pl.BlockSpec((1, H_B, PAGE_SIZE), per_b),
                  pl.BlockSpec((None, n_phys, H_B, PAGE_SIZE), lambda b, j, pt: (layer, 0, 0, 0))]
        + kv_specs + kv_specs,
        out_specs=pl.BlockSpec((1, n_q, C_B), per_b),
        scratch_shapes=[pltpu.VMEM((N_ROWS, C_B), BF16), pltpu.VMEM((N_ROWS, 1), F32),
                        pltpu.VMEM((N_ROWS, 1), F32), pltpu.VMEM((N_ROWS, C_B), F32),
                        pltpu.VMEM((H_B, PAGE_SIZE), F32), pltpu.VMEM((N_ROWS, 1), F32)],
    )
    return pl.pallas_call(
        kern,
        grid_spec=grid_spec,
        out_shape=jax.ShapeDtypeStruct((db, n_q, C_B), F32),
        compiler_params=_cparams("parallel", "arbitrary"),
        name="fox_paged",
    )(page_table, q32, kn_pad, vn_pad, lfn_pad, cache_lf, *([cache_k] * pp), *([cache_v] * pp))


CONV_PAD = 32
CONV_ROWS = 256


def _conv_kernel(z_ref, buf_ref, w_ref, b_ref, g_ref, beta_ref, y_ref, nb_ref, xp_scr, sh_scr, *, t):
    z = z_ref[0]
    u = z[:, :C_CONV] * _sigmoid(z[:, C_CONV:])
    off = CONV_PAD - (CONV_W - 1)
    xp_scr[off:CONV_PAD, :] = buf_ref[0]
    xp_scr[CONV_PAD:CONV_PAD + t, :] = u
    nb_ref[0] = xp_scr[t + off:t + CONV_PAD, :]
    rt = min(t, CONV_ROWS)
    sub = 8
    for r0 in range(0, t, rt):
        acc = jnp.zeros((rt, C_CONV), F32) + b_ref[...]
        for s in range(sub):
            taps = [j for j in range(CONV_W) if (off + j) % sub == s]
            if not taps:
                continue
            span = sub * max((off + j) // sub for j in taps) + rt
            if s == 0:
                src, base = xp_scr, r0
            else:
                sh_scr[0:span, :] = xp_scr[r0 + s:r0 + s + span, :]
                src, base = sh_scr, 0
            for j in taps:
                q0 = base + sub * ((off + j) // sub)
                acc = acc + w_ref[j:j + 1, :] * src[q0:q0 + rt, :]
        mean = jnp.mean(acc, axis=-1, keepdims=True)
        xc = acc - mean
        var = jnp.mean(xc * xc, axis=-1, keepdims=True)
        yn = xc * lax.rsqrt(var + LN_EPS) * g_ref[...] + beta_ref[...]
        y_ref[0, r0:r0 + rt, :] = (yn * _sigmoid(yn)).astype(y_ref.dtype)


def _conv_module(z3, buf, w, bias, ln_g, ln_b, out_dtype=BF16):
    b, t, _ = z3.shape
    fix = lambda i: (0, 0)
    kern = functools.partial(_conv_kernel, t=t)
    return pl.pallas_call(
        kern,
        grid=(b,),
        in_specs=[pl.BlockSpec((1, t, 2 * C_CONV), lambda i: (i, 0, COL_C // (2 * C_CONV))),
                  pl.BlockSpec((1, CONV_W - 1, C_CONV), lambda i: (i, 0, 0)),
                  pl.BlockSpec((CONV_W, C_CONV), fix), pl.BlockSpec((1, C_CONV), fix),
                  pl.BlockSpec((1, C_CONV), fix), pl.BlockSpec((1, C_CONV), fix)],
        out_specs=[pl.BlockSpec((1, t, C_CONV), lambda i: (i, 0, 0)),
                   pl.BlockSpec((1, CONV_W - 1, C_CONV), lambda i: (i, 0, 0))],
        out_shape=[jax.ShapeDtypeStruct((b, t, C_CONV), out_dtype),
                   jax.ShapeDtypeStruct((b, CONV_W - 1, C_CONV), F32)],
        scratch_shapes=[pltpu.VMEM((t + CONV_PAD, C_CONV), F32),
                        pltpu.VMEM((min(t, CONV_ROWS) + CONV_PAD, C_CONV), F32)],
        compiler_params=_cparams("parallel"),
        name="conv_module",
    )(z3, buf, w, bias.reshape(1, -1), ln_g.reshape(1, -1), ln_b.reshape(1, -1))


def _pre_attn_kernel(x_ref, ya_ref, yb_ref, yc_ref, woa_ref, wob_ref, woc_ref, gx_ref, wq_ref, qn_ref,
                     x1_ref, q_ref):
    x1 = (x_ref[...] + jnp.dot(ya_ref[...], woa_ref[...], preferred_element_type=F32)
          + jnp.dot(yb_ref[...], wob_ref[...], preferred_element_type=F32)
          + jnp.dot(yc_ref[...], woc_ref[...], preferred_element_type=F32))
    x1_ref[...] = x1
    q = jnp.dot(_rms(x1, gx_ref[...]).astype(BF16), wq_ref[...], preferred_element_type=F32)
    for h in range(H_X):
        sl = slice(h * HEAD_DIM_X, (h + 1) * HEAD_DIM_X)
        q_ref[:, sl] = (_rms(q[:, sl], qn_ref[...]) * (HEAD_DIM_X ** -0.5)).astype(q_ref.dtype)


def _pre_attn(x2d, ya, yb, yc, wl, tm):
    m, d = x2d.shape
    row = lambda i: (i, 0)
    fix = lambda i: (0, 0)
    return pl.pallas_call(
        _pre_attn_kernel,
        grid=(m // tm,),
        in_specs=[pl.BlockSpec((tm, d), row), pl.BlockSpec((tm, C_A), row), pl.BlockSpec((tm, C_B), row),
                  pl.BlockSpec((tm, C_CONV), row),
                  pl.BlockSpec((C_A, d), fix), pl.BlockSpec((C_B, d), fix), pl.BlockSpec((C_CONV, d), fix),
                  pl.BlockSpec((1, d), fix), pl.BlockSpec((d, d), fix), pl.BlockSpec((1, HEAD_DIM_X), fix)],
        out_specs=[pl.BlockSpec((tm, d), row), pl.BlockSpec((tm, d), row)],
        out_shape=[jax.ShapeDtypeStruct((m, d), F32), jax.ShapeDtypeStruct((m, d), BF16)],
        compiler_params=_cparams("parallel"),
        name="pre_attn",
    )(x2d, ya, yb, yc, wl['w_out_a'], wl['w_out_b'], wl['w_out_c'], wl['g_x'].reshape(1, d),
      wl['w_xq_bf'], wl['xq_norm'].reshape(1, HEAD_DIM_X))


MEM_LANE_CHUNKS = HEAD_DIM_X // LANE
MEM_ROWS_PER_TOKEN = H_X * MEM_LANE_CHUNKS


def _xattn_kernel(q_ref, k_ref, v_ref, o_ref, *, row_major):
    q = q_ref[0]

    def head_of(ref, h):
        if row_major:
            return ref[0, :, h * HEAD_DIM_X:(h + 1) * HEAD_DIM_X].astype(BF16)
        parts = [ref[0, pl.ds(c * H_X + h, N_MEM, stride=MEM_ROWS_PER_TOKEN), :]
                 for c in range(MEM_LANE_CHUNKS)]
        return jnp.concatenate(parts, axis=1).astype(BF16)

    heads = range(H_X)
    s = [lax.dot_general(q[:, h * HEAD_DIM_X:(h + 1) * HEAD_DIM_X], head_of(k_ref, h),
                         (((1,), (1,)), ((), ())), preferred_element_type=F32) for h in heads]
    m = [jnp.max(s[h], axis=1, keepdims=True) for h in heads]
    p = [jnp.exp(s[h] - m[h]) for h in heads]
    p = [p[h] / jnp.sum(p[h], axis=1, keepdims=True) for h in heads]
    o = [jnp.dot(p[h].astype(BF16), head_of(v_ref, h), preferred_element_type=F32) for h in heads]
    o_ref[0] = jnp.concatenate(o, axis=1).astype(o_ref.dtype)


def _xattn(q3, mk, mv, tq, row_major=True, layer=None):
    b, t, d = q3.shape
    if layer is None:
        kv_spec = pl.BlockSpec((1,) + mk.shape[1:], lambda i, j: (i, 0, 0))
    else:
        kv_spec = pl.BlockSpec((None, 1) + mk.shape[2:], lambda i, j: (layer, i, 0, 0))
    return pl.pallas_call(
        functools.partial(_xattn_kernel, row_major=row_major),
        grid=(b, t // tq),
        in_specs=[pl.BlockSpec((1, tq, d), lambda i, j: (i, j, 0)), kv_spec, kv_spec],
        out_specs=pl.BlockSpec((1, tq, d), lambda i, j: (i, j, 0)),
        out_shape=jax.ShapeDtypeStruct((b, t, d), BF16),
        compiler_params=_cparams("parallel", "parallel"),
        name="xattn",
    )(q3, mk, mv)


def _mem_device_rows(a):
    n_l, b = a.shape[:2]
    a = a.reshape(n_l, b, N_MEM, H_X, MEM_LANE_CHUNKS, LANE)
    return jnp.transpose(a, (0, 1, 2, 4, 3, 5)).reshape(n_l, b, N_MEM * MEM_ROWS_PER_TOKEN, LANE)


FFN_TN = 256
FFN_CHUNKS = D_FF // FFN_TN


def _ffn_kernel(*refs, tm, tiles_per_seq, seg, carried):
    if carried:
        (x1_ref, o_ref, wo_ref, g_ref, wg_ref, wv_ref, cwg_ref, cwv_ref, cbg_ref, cbv_ref, wd_ref,
         y_ref, hg_ref, hv_ref, x2_scr, xn_scr, acc_scr, cg_scr, cv_scr) = refs
    else:
        (x1_ref, o_ref, wo_ref, g_ref, wg_ref, wv_ref, cwg_ref, cwv_ref, cbg_ref, cbv_ref, wd_ref,
         p1g_ref, p1v_ref, p2g_ref, p2v_ref,
         y_ref, hg_ref, hv_ref, x2_scr, xn_scr, acc_scr) = refs
    mi = pl.program_id(0)
    c = pl.program_id(1)

    @pl.when(c == 0)
    def _():
        x2 = x1_ref[...] + jnp.dot(o_ref[...], wo_ref[...], preferred_element_type=F32)
        x2_scr[...] = x2
        xn_scr[...] = _rms(x2, g_ref[...]).astype(BF16)
        acc_scr[...] = jnp.zeros_like(acc_scr)

    xn = xn_scr[...]
    row = _iota((tm, 1), 0)
    halves = []
    for name in ("g", "v"):
        w_ref, cw_ref, cb_ref = (wg_ref, cwg_ref, cbg_ref) if name == "g" else (wv_ref, cwv_ref, cbv_ref)
        h = jnp.dot(xn, w_ref[...], preferred_element_type=F32)
        hm1 = pltpu.roll(h, 1, axis=0)
        hm2 = pltpu.roll(h, 2, axis=0)
        if carried:
            c_scr = cg_scr if name == "g" else cv_scr
            prev = jnp.where(mi % tiles_per_seq == 0, 0.0, c_scr[c])
            hm1 = jnp.where(row == 0, prev[7:8, :], hm1)
            hm2 = jnp.where(row == 0, prev[6:7, :], jnp.where(row == 1, prev[7:8, :], hm2))
            c_scr[c] = h[tm - 8:tm, :]
            out_ref = hg_ref if name == "g" else hv_ref
            out_ref[0] = h[tm - 8:tm, :]
        else:
            p1 = (p1g_ref if name == "g" else p1v_ref)[...]
            p2 = (p2g_ref if name == "g" else p2v_ref)[...]
            hm1 = jnp.where(row % seg == 0, p1, hm1)
            hm2 = jnp.where(row % seg < 2, p2, hm2)
            out_ref = hg_ref if name == "g" else hv_ref
            out_ref[...] = h
        halves.append(cw_ref[0:1, :] * hm2 + cw_ref[1:2, :] * hm1 + cw_ref[2:3, :] * h + cb_ref[...])
    gate, val = halves
    act = gate * _sigmoid(gate) * val
    acc_scr[...] += jnp.dot(act.astype(BF16), wd_ref[...], preferred_element_type=F32)

    @pl.when(c == FFN_CHUNKS - 1)
    def _():
        y_ref[...] = x2_scr[...] + acc_scr[...]


def _ffn(x1, o, wl, *, tm, tiles_per_seq=1, seg=1, prev=None):
    m, d = x1.shape
    carried = prev is None
    tn = FFN_TN
    row = lambda i, c: (i, 0)
    fix = lambda i, c: (0, 0)
    gcol = lambda i, c: (0, c)
    vcol = lambda i, c: (0, c + FFN_CHUNKS)
    in_specs = [pl.BlockSpec((tm, d), row), pl.BlockSpec((tm, d), row), pl.BlockSpec((d, d), fix),
                pl.BlockSpec((1, d), fix),
                pl.BlockSpec((d, tn), gcol), pl.BlockSpec((d, tn), vcol),
                pl.BlockSpec((FFN_CONV_W, tn), gcol), pl.BlockSpec((FFN_CONV_W, tn), vcol),
                pl.BlockSpec((1, tn), gcol), pl.BlockSpec((1, tn), vcol),
                pl.BlockSpec((tn, d), lambda i, c: (c, 0))]
    args = [x1, o, wl['w_xo_bf'], wl['g_ffn'].reshape(1, d), wl['w_up_bf'], wl['w_up_bf'],
            wl['ffn_conv_w'], wl['ffn_conv_w'], wl['ffn_conv_b'].reshape(1, -1),
            wl['ffn_conv_b'].reshape(1, -1), wl['w_down_bf']]
    scratch = [pltpu.VMEM((tm, d), F32), pltpu.VMEM((tm, d), BF16), pltpu.VMEM((tm, d), F32)]
    if carried:
        h_shape = jax.ShapeDtypeStruct((m // tm, 8, D_FF), F32)
        h_spec = pl.BlockSpec((1, 8, tn), lambda i, c: (i, 0, c))
        scratch += [pltpu.VMEM((FFN_CHUNKS, 8, tn), F32)] * 2
    else:
        p1, p2 = prev
        in_specs += [pl.BlockSpec((tm, tn), lambda i, c: (i, c)),
                     pl.BlockSpec((tm, tn), lambda i, c: (i, c + FFN_CHUNKS))] * 2
        args += [p1, p1, p2, p2]
        h_shape = jax.ShapeDtypeStruct((m, D_FF), F32)
        h_spec = pl.BlockSpec((tm, tn), lambda i, c: (i, c))
    kern = functools.partial(_ffn_kernel, tm=tm, tiles_per_seq=tiles_per_seq, seg=seg, carried=carried)
    return pl.pallas_call(
        kern,
        grid=(m // tm, FFN_CHUNKS),
        in_specs=in_specs,
        out_specs=[pl.BlockSpec((tm, d), row), h_spec, h_spec],
        out_shape=[jax.ShapeDtypeStruct((m, d), F32), h_shape, h_shape],
        scratch_shapes=scratch,
        compiler_params=_cparams("arbitrary", "arbitrary"),
        name="ffn",
    )(*args)


def _prep_weights(w, l):
    wl = {n: a[l] for n, a in w.items()}
    w_in = wl['w_in']
    o_b = RWKV_COLS
    o_c = RWKV_COLS + 3 * C_B + H_B
    pad = jnp.zeros((D_MODEL, LANE - H_B), F32)
    w_in2 = jnp.concatenate([w_in[:, :o_b], w_in[:, o_b:o_b + 3 * C_B], w_in[:, o_c:],
                             w_in[:, o_b + 3 * C_B:o_c], pad], axis=1)
    wl['w_in_bf'] = w_in2.astype(BF16)
    for n in ('rwkv_w_up', 'rwkv_a_up', 'rwkv_g_up', 'w_xq', 'w_xk', 'w_xv', 'w_xo', 'w_up', 'w_down'):
        wl[n + '_bf'] = wl[n].astype(BF16)
    w_out = wl['w_out'].astype(BF16)
    wl['w_out_a'] = w_out[:C_A]
    wl['w_out_b'] = w_out[C_A:C_A + C_B]
    wl['w_out_c'] = w_out[C_A + C_B:]
    wl['fox_q_norm_t'] = jnp.tile(wl['fox_q_norm'], H_B).reshape(1, C_B)
    wl['fox_k_norm_t'] = jnp.tile(wl['fox_k_norm'], H_B).reshape(1, C_B)
    wl['fox_b_f_pad'] = jnp.concatenate([wl['fox_b_f'], jnp.zeros((LANE - H_B,), F32)]).reshape(1, LANE)
    return wl


def _pick(n, prefs):
    for p in prefs:
        if n % p == 0:
            return p
    return n


def _prompt_layer(x, mem2d, wl):
    b, t, d = x.shape
    m = b * t
    tm = _pick(m, (512, 256, 128, 64, 32, 16, 8))
    mk, mv = _mem_kv(mem2d, wl['g_mem'], wl['w_xk_bf'], wl['w_xv_bf'], wl['xk_norm'],
                     _pick(mem2d.shape[0], (512, 256)))
    z2 = _norm_matmul(x.reshape(m, d), wl['g_mix'], wl['w_in_bf'], tm)
    z3 = z2.reshape(b, t, IN_COLS_PAD)
    tb = _pick(t, (512, 256, 128, 64))
    ya, s_t = _rwkv(z3, jnp.zeros((b, 1, RWKV_COLS), F32), jnp.zeros((b, H_A, HEAD_DIM_A, HEAD_DIM_A), F32),
                    wl, tb=tb, chunk=min(RWKV_CHUNK, tb), t_valid=tb, unroll=RWKV_UNROLL)
    tp = _pick(t, (FOX_TILE, 256, 128))
    kt, vt, lft, qat, ka, vat = _fox_prep(z3, wl['fox_q_norm_t'], wl['fox_k_norm_t'], wl['fox_b_f_pad'], tp)
    yb = _fox_attn(qat, ka, vat)
    yc, conv_buf = _conv_module(z3, jnp.zeros((b, CONV_W - 1, C_CONV), F32), wl['conv_w'], wl['conv_b'],
                                wl['conv_ln_g'], wl['conv_ln_b'])
    x1, q = _pre_attn(x.reshape(m, d), ya.reshape(m, C_A), yb.reshape(m, C_B), yc.reshape(m, C_CONV), wl, tm)
    o = _xattn(q.reshape(b, t, d), mk.reshape(b, N_MEM, d), mv.reshape(b, N_MEM, d),
               _pick(t, (512, 256, 128, 64)))
    tf = _pick(t, (1024, 512, 256, 128, 64))
    y, hg, hv = _ffn(x1, o.reshape(m, d), wl, tm=tf, tiles_per_seq=t // tf)
    tiles = t // tf
    ffn_buf = jnp.concatenate([hg.reshape(b, tiles, 8, D_FF)[:, -1, 6:], hv.reshape(b, tiles, 8, D_FF)[:, -1, 6:]],
                              axis=-1)
    st = (jnp.transpose(kt.reshape(b, H_B, HEAD_DIM_B, t), (0, 3, 1, 2)),
          jnp.transpose(vt.reshape(b, H_B, HEAD_DIM_B, t), (0, 3, 1, 2)),
          jnp.transpose(lft, (0, 2, 1)),
          s_t,
          z3[:, t - 1:t, :RWKV_COLS],
          conv_buf,
          ffn_buf)
    return y.reshape(b, t, d), st, mk.reshape(b, N_MEM, H_X, HEAD_DIM_X), mv.reshape(b, N_MEM, H_X, HEAD_DIM_X)


def _sample_layer(x, wl, layer, cache_k, cache_v, cache_lf, page_table, s0, shift0, conv0, ffn0, mk, mv):
    b, t, d = x.shape
    m = b * t
    z2 = _norm_matmul(x.reshape(m, d), wl['g_mix'], wl['w_in_bf'], m)
    z3 = z2.reshape(b, t, IN_COLS_PAD)
    t_pad = 8
    za = jnp.pad(z3[:, :, :RWKV_COLS], ((0, 0), (0, t_pad - t), (0, 0)))
    seqs = _pick(b, (8, 4, 2, 1))
    ya, s_t = _rwkv(za, shift0, s0, wl, tb=t_pad * seqs, chunk=t_pad, t_valid=t, unroll=RWKV_UNROLL,
                    seqs=seqs, out_dtype=F32)
    ya = ya[:, :t].astype(BF16)
    q, k_n, lf = _fox_norm(z2, wl['fox_q_norm_t'], wl['fox_k_norm_t'], wl['fox_b_f_pad'])
    zv = z3[:, :, COL_V:COL_V + C_B]
    page_t = lambda a: jnp.pad(jnp.swapaxes(a, 1, 2), ((0, 0), (0, 0), (0, PAGE_SIZE - t)))
    q32 = jnp.repeat(q.reshape(b, t, C_B), H_B, axis=1)
    yb = _fox_paged(page_table, q32, page_t(k_n.reshape(b, t, C_B)), page_t(zv),
                    page_t(lf.reshape(b, t, LANE)[:, :, :H_B]), cache_k, cache_v, cache_lf, layer, t)
    yb = yb.astype(BF16)
    yc, conv_buf = _conv_module(z3, conv0, wl['conv_w'], wl['conv_b'], wl['conv_ln_g'], wl['conv_ln_b'], F32)
    yc = yc.astype(BF16)
    x1, q = _pre_attn(x.reshape(m, d), ya.reshape(m, C_A), yb.reshape(m, C_B), yc.reshape(m, C_CONV), wl, m)
    t16 = 16
    qp = jnp.pad(q.reshape(b, t, d), ((0, 0), (0, t16 - t), (0, 0)))
    o = _xattn(qp, mk, mv, t16, row_major=False, layer=layer)[:, :t]
    zrow = jnp.zeros((b, 1, 2 * D_FF), F32)
    p1 = jnp.concatenate([ffn0[:, 1:2], jnp.tile(zrow, (1, t - 1, 1))], axis=1).reshape(m, 2 * D_FF)
    p2 = jnp.concatenate([ffn0, jnp.tile(zrow, (1, t - 2, 1))], axis=1).reshape(m, 2 * D_FF)
    y, hg, hv = _ffn(x1, o.reshape(m, d), wl, tm=m, seg=t, prev=(p1, p2))
    ffn_buf = jnp.concatenate([hg.reshape(b, t, D_FF)[:, t - 2:], hv.reshape(b, t, D_FF)[:, t - 2:]], axis=-1)
    st = (k_n.reshape(b, t, H_B, HEAD_DIM_B), zv.reshape(b, t, H_B, HEAD_DIM_B),
          lf.reshape(b, t, LANE)[:, :, :H_B], s_t, z3[:, t - 1:t, :RWKV_COLS], conv_buf, ffn_buf)
    return y.reshape(b, t, d), st


def kernel(x_prompt, x_sample, mem_prompt, cache_fox_k, cache_fox_v, cache_fox_logf, page_table, state_rwkv, state_rwkv_shift, state_conv, state_ffn, cache_mem_k, cache_mem_v, g_mix, w_in, rwkv_mu, rwkv_w0, rwkv_w_up, rwkv_a0, rwkv_a_up, rwkv_g_up, rwkv_k_k, rwkv_k_a, rwkv_r_k, rwkv_ln_g, rwkv_ln_b, fox_q_norm, fox_k_norm, fox_b_f, conv_w, conv_b, conv_ln_g, conv_ln_b, w_out, g_x, g_mem, w_xq, w_xk, w_xv, xq_norm, xk_norm, w_xo, g_ffn, w_up, ffn_conv_w, ffn_conv_b, w_down):
    w = dict(g_mix=g_mix, w_in=w_in, rwkv_mu=rwkv_mu, rwkv_w0=rwkv_w0, rwkv_w_up=rwkv_w_up, rwkv_a0=rwkv_a0,
             rwkv_a_up=rwkv_a_up, rwkv_g_up=rwkv_g_up, rwkv_k_k=rwkv_k_k, rwkv_k_a=rwkv_k_a, rwkv_r_k=rwkv_r_k,
             rwkv_ln_g=rwkv_ln_g, rwkv_ln_b=rwkv_ln_b, fox_q_norm=fox_q_norm, fox_k_norm=fox_k_norm,
             fox_b_f=fox_b_f, conv_w=conv_w, conv_b=conv_b, conv_ln_g=conv_ln_g, conv_ln_b=conv_ln_b,
             w_out=w_out, g_x=g_x, g_mem=g_mem, w_xq=w_xq, w_xk=w_xk, w_xv=w_xv, xq_norm=xq_norm,
             xk_norm=xk_norm, w_xo=w_xo, g_ffn=g_ffn, w_up=w_up, ffn_conv_w=ffn_conv_w,
             ffn_conv_b=ffn_conv_b, w_down=w_down)
    depth = w_in.shape[0]
    wls = [_prep_weights(w, l) for l in range(depth)]

    b, n_mem, d = mem_prompt.shape
    mem2d = mem_prompt.reshape(b * n_mem, d)
    x = x_prompt
    p_st, p_mk, p_mv = [], [], []
    for l in range(depth):
        x, st, mk, mv = _prompt_layer(x, mem2d, wls[l])
        p_st.append(st)
        p_mk.append(mk)
        p_mv.append(mv)
    y_prompt = x
    p_out = tuple(jnp.stack(f) for f in zip(*p_st))

    ck = jnp.transpose(cache_fox_k, (0, 1, 3, 4, 2))
    cv = jnp.transpose(cache_fox_v, (0, 1, 3, 4, 2))
    clf = jnp.transpose(cache_fox_logf, (0, 1, 3, 2))
    mem_k_rows = _mem_device_rows(cache_mem_k)
    mem_v_rows = _mem_device_rows(cache_mem_v)
    x = x_sample
    s_st = []
    for l in range(depth):
        x, st = _sample_layer(x, wls[l], l, ck, cv, clf, page_table, state_rwkv[l],
                              state_rwkv_shift[l], state_conv[l], state_ffn[l], mem_k_rows, mem_v_rows)
        s_st.append(st)
    y_sample = x
    s_out = tuple(jnp.stack(f) for f in zip(*s_st))

    return (y_prompt, y_sample) + p_out + (jnp.stack(p_mk), jnp.stack(p_mv)) + s_out
```

```python
import functools

import jax
import jax.numpy as jnp
from jax import lax
from jax.experimental import pallas as pl
from jax.experimental.pallas import tpu as pltpu

F32 = jnp.float32
BF16 = jnp.bfloat16

D_MODEL = 1024
HEAD_DIM_A = 64
C_A = 256
H_A = 4
W_LORA = 64
A_LORA = 64
G_LORA = 128
RWKV_COLS = 3 * C_A + W_LORA + A_LORA + G_LORA
HEAD_DIM_B = 64
C_B = 512
H_B = 8
C_CONV = 256
CONV_W = 31
N_MEM = 256
H_X = 4
HEAD_DIM_X = 256
D_FF = 2816
FFN_CONV_W = 3
PAGE_SIZE = 128
RMS_EPS = 1e-6
LN_EPS = 1e-5
GN_EPS = 64e-5

COL_Q = RWKV_COLS
COL_K = COL_Q + C_B
COL_V = COL_K + C_B
COL_C = COL_V + C_B
COL_F = COL_C + 2 * C_CONV
LANE = 128
IN_COLS_PAD = COL_F + LANE

VMEM_LIMIT_BYTES = 56 * 1024 * 1024
NEG_BIG = -1e30


def _cparams(*sem):
    return pltpu.CompilerParams(dimension_semantics=sem, vmem_limit_bytes=VMEM_LIMIT_BYTES)


def _dot(a, b):
    return jnp.dot(a.astype(BF16), b.astype(BF16), preferred_element_type=F32)


def _dot_nt(a, b):
    return lax.dot_general(a.astype(BF16), b.astype(BF16), (((1,), (1,)), ((), ())),
                           preferred_element_type=F32)


def _dot_tn(a, b):
    return lax.dot_general(a.astype(BF16), b.astype(BF16), (((0,), (0,)), ((), ())),
                           preferred_element_type=F32)


def _split_bf16(x, n):
    parts, r = [], x
    for i in range(n):
        p = r.astype(BF16)
        parts.append(p)
        if i + 1 < n:
            r = r - p.astype(F32)
    return parts


def _dot_split_lhs(x, m_bf, n=3):
    acc = None
    for p in _split_bf16(x, n):
        d = jnp.dot(p, m_bf, preferred_element_type=F32)
        acc = d if acc is None else acc + d
    return acc


def _dot_split_rhs(m_bf, x, n=3):
    acc = None
    for p in _split_bf16(x, n):
        d = jnp.dot(m_bf, p, preferred_element_type=F32)
        acc = d if acc is None else acc + d
    return acc


def _dot_nt_split_rhs(m_bf, x, n=3):
    acc = None
    for p in _split_bf16(x, n):
        d = lax.dot_general(m_bf, p, (((1,), (1,)), ((), ())), preferred_element_type=F32)
        acc = d if acc is None else acc + d
    return acc


def _iota(shape, axis):
    return lax.broadcasted_iota(jnp.int32, shape, axis)


def _block_ones(n, seg):
    return (_iota((n, n), 0) // seg == _iota((n, n), 1) // seg).astype(BF16)


def _sigmoid(x):
    return 1.0 / (1.0 + jnp.exp(-x))


def _softplus(x):
    return jnp.maximum(x, 0.0) + jnp.log(1.0 + jnp.exp(-jnp.abs(x)))


def _rms(x, g):
    ms = jnp.mean(x * x, axis=-1, keepdims=True)
    return x * lax.rsqrt(ms + RMS_EPS) * g


def _norm_matmul_kernel(x_ref, g_ref, w_ref, o_ref):
    xn = _rms(x_ref[...], g_ref[...])
    o_ref[...] = jnp.dot(xn.astype(BF16), w_ref[...], preferred_element_type=F32)


def _norm_matmul(x2d, g, w_bf, tm):
    m, d = x2d.shape
    n = w_bf.shape[1]
    return pl.pallas_call(
        _norm_matmul_kernel,
        grid=(m // tm,),
        in_specs=[pl.BlockSpec((tm, d), lambda i: (i, 0)),
                  pl.BlockSpec((1, d), lambda i: (0, 0)),
                  pl.BlockSpec((d, n), lambda i: (0, 0))],
        out_specs=pl.BlockSpec((tm, n), lambda i: (i, 0)),
        out_shape=jax.ShapeDtypeStruct((m, n), F32),
        compiler_params=_cparams("parallel"),
        name="norm_matmul",
    )(x2d, g.reshape(1, d), w_bf)


def _mem_kv_kernel(m_ref, g_ref, wk_ref, wv_ref, kn_ref, k_out, v_out):
    xn = _rms(m_ref[...], g_ref[...]).astype(BF16)
    kraw = jnp.dot(xn, wk_ref[...], preferred_element_type=F32)
    v_out[...] = jnp.dot(xn, wv_ref[...], preferred_element_type=F32)
    for h in range(H_X):
        sl = slice(h * HEAD_DIM_X, (h + 1) * HEAD_DIM_X)
        k_out[:, sl] = _rms(kraw[:, sl], kn_ref[...])


def _mem_kv(mem2d, g, wk_bf, wv_bf, k_norm, tm):
    m, d = mem2d.shape
    row = lambda i: (i, 0)
    fix = lambda i: (0, 0)
    return pl.pallas_call(
        _mem_kv_kernel,
        grid=(m // tm,),
        in_specs=[pl.BlockSpec((tm, d), row), pl.BlockSpec((1, d), fix),
                  pl.BlockSpec((d, d), fix), pl.BlockSpec((d, d), fix),
                  pl.BlockSpec((1, HEAD_DIM_X), fix)],
        out_specs=[pl.BlockSpec((tm, d), row), pl.BlockSpec((tm, d), row)],
        out_shape=[jax.ShapeDtypeStruct((m, d), F32)] * 2,
        compiler_params=_cparams("parallel"),
        name="mem_kv",
    )(mem2d, g.reshape(1, d), wk_bf, wv_bf, k_norm.reshape(1, HEAD_DIM_X))


RWKV_CHUNK = 128
RWKV_UNROLL = 2

def _rwkv_kernel(z_ref, shift_ref, s0_ref, mu_ref, w0_ref, wup_ref, a0_ref, aup_ref, gup_ref,
                 kk_ref, ka_ref, rk_ref, lng_ref, lnb_ref,
                 y_ref, st_ref,
                 s_scr, zprev_scr, lw_scr, kn_scr, be_scr, km_scr, r_scr, v_scr, yacc_scr,
                 rhat_scr, gend_scr, glr_scr, dadd_scr,
                 *, tb, chunk, t_valid, unroll, seqs):
    ti = pl.program_id(1)
    row = _iota((tb, 1), 0)
    if seqs == 1:
        @pl.when(ti == 0)
        def _():
            s_scr[...] = s0_ref[0]
            zprev_scr[...] = shift_ref[0]

        z = z_ref[0]
        pos = row
        zprev = jnp.where(row == 0, zprev_scr[...], pltpu.roll(z, 1, axis=0))
        zprev_scr[...] = z[tb - 1:tb, :]
    else:
        z = z_ref[...].reshape(tb, RWKV_COLS)
        pos = row % chunk
        before = jnp.broadcast_to(shift_ref[...], (seqs, chunk, RWKV_COLS)).reshape(tb, RWKV_COLS)
        zprev = jnp.where(pos == 0, before, pltpu.roll(z, 1, axis=0))
    zs = z + (zprev - z) * mu_ref[...]
    r = zs[:, 0:C_A]
    k = zs[:, C_A:2 * C_A]
    v = zs[:, 2 * C_A:3 * C_A]
    o_w = 3 * C_A
    wd = zs[:, o_w:o_w + W_LORA]
    ad = zs[:, o_w + W_LORA:o_w + W_LORA + A_LORA]
    gd = zs[:, o_w + W_LORA + A_LORA:RWKV_COLS]
    w_log = -_softplus(-(w0_ref[...] + _dot(jnp.tanh(wd), wup_ref[...]))) - 0.5
    lw = -jnp.exp(w_log)
    a = _sigmoid(a0_ref[...] + _dot(ad, aup_ref[...]))
    g = _dot(_sigmoid(gd), gup_ref[...])
    k_mod = k * (1.0 + (a - 1.0) * ka_ref[...])
    bd = _block_ones(C_A, HEAD_DIM_A)
    kk0 = k * kk_ref[...]
    kn = kk0 / jnp.maximum(jnp.sqrt(_dot_split_lhs(kk0 * kk0, bd)), 1e-12)
    if t_valid < tb // seqs:
        ok = pos < t_valid
        lw = jnp.where(ok, lw, 0.0)
        kn = jnp.where(ok, kn, 0.0)
        k_mod = jnp.where(ok, k_mod, 0.0)
    lw_scr[...] = lw
    kn_scr[...] = kn
    be_scr[...] = kn * a
    km_scr[...] = k_mod
    r_scr[...] = r
    v_scr[...] = v

    c = chunk
    ri = _iota((c, c), 0)
    ci = _iota((c, c), 1)
    lincl = (ri >= ci).astype(BF16)
    m_strict = ri > ci
    m_incl = ri >= ci
    eye = (ri == ci).astype(F32)

    heads = range(H_A)
    hsl = [slice(h * HEAD_DIM_A, (h + 1) * HEAD_DIM_A) for h in heads]

    def chunk_terms(j, carry):
        rows = pl.ds(pl.multiple_of(j * c, c), c)
        lw_c = lw_scr[rows, :]
        cum = _dot_split_rhs(lincl, lw_c)
        g_inc = jnp.exp(cum)
        g_exc = jnp.exp(cum - lw_c)
        g_inv = jnp.exp(-cum)
        g_end = g_inc[c - 1:c, :]
        at = -kn_scr[rows, :] * g_exc
        bt = be_scr[rows, :] * g_inv
        kt = km_scr[rows, :] * g_inv
        rt = r_scr[rows, :] * g_inc
        vv = v_scr[rows, :]
        btc = bt * g_end
        ktc = kt * g_end
        a_ab = [jnp.where(m_strict, _dot_nt(at[:, s], bt[:, s]), 0.0) for s in hsl]
        a_ak = [jnp.where(m_strict, _dot_nt(at[:, s], kt[:, s]), 0.0) for s in hsl]
        m_rb = [jnp.where(m_incl, _dot_nt(rt[:, s], bt[:, s]), 0.0) for s in hsl]
        m_rk = [jnp.where(m_incl, _dot_nt(rt[:, s], kt[:, s]), 0.0) for s in hsl]
        x = [eye + a for a in a_ab]
        p = a_ab
        n = 1
        while 2 * n < c:
            p = [_dot(q, q) for q in p]
            x = [xi + _dot(pi, xi) for xi, pi in zip(x, p)]
            n *= 2
        a_hat = [_dot(x[h], at[:, hsl[h]]) for h in heads]
        u0 = [_dot(x[h], _dot(a_ak[h], vv[:, hsl[h]])) for h in heads]
        r_hat = [rt[:, hsl[h]] + _dot(m_rb[h], a_hat[h]) for h in heads]
        y0 = [_dot(m_rb[h], u0[h]) + _dot(m_rk[h], vv[:, hsl[h]]) for h in heads]
        g_lr = [_dot_tn(a_hat[h], btc[:, hsl[h]]) for h in heads]
        d_add = [_dot_tn(u0[h], btc[:, hsl[h]]) + _dot_tn(vv[:, hsl[h]], ktc[:, hsl[h]]) for h in heads]
        rhat_scr[rows, :] = jnp.concatenate(r_hat, axis=1)
        yacc_scr[rows, :] = jnp.concatenate(y0, axis=1)
        gend_scr[j] = jnp.broadcast_to(g_end, (8, C_A))
        for h in heads:
            glr_scr[j, h] = g_lr[h]
            dadd_scr[j, h] = d_add[h]
        return carry

    lax.fori_loop(0, tb // c, chunk_terms, 0, unroll=min(unroll, tb // c))

    def chunk_state(j, carry):
        rows = pl.ds(pl.multiple_of(j * c, c), c)
        s = [s_scr[h] if seqs == 1 else s0_ref[j, h] for h in heads]
        r_hat = rhat_scr[rows, :]
        g_end = gend_scr[j][0:1, :]
        y = [_dot_nt(r_hat[:, hsl[h]], s[h]) for h in heads]
        s_new = [s[h] * g_end[:, hsl[h]] + _dot(s[h], glr_scr[j, h]) + dadd_scr[j, h] for h in heads]
        yacc_scr[rows, :] = yacc_scr[rows, :] + jnp.concatenate(y, axis=1)
        for h in heads:
            if seqs == 1:
                s_scr[h] = s_new[h]
            else:
                st_ref[j, h] = s_new[h]
        return carry

    lax.fori_loop(0, tb // c, chunk_state, 0)

    y = yacc_scr[...]
    inv_n = 1.0 / HEAD_DIM_A
    mean = _dot_split_lhs(y, bd) * inv_n
    yc = y - mean
    var = _dot_split_lhs(yc * yc, bd) * inv_n
    yn = yc * lax.rsqrt(var + GN_EPS) * lng_ref[...] + lnb_ref[...]
    bonus = _dot_split_lhs(r * k_mod * rk_ref[...], bd) * v
    out = ((yn + bonus) * g).astype(y_ref.dtype)
    if seqs == 1:
        y_ref[0] = out

        @pl.when(ti == pl.num_programs(1) - 1)
        def _():
            st_ref[0] = s_scr[...]
    else:
        y_ref[...] = out.reshape(seqs, chunk, C_A)


def _rwkv(z3, shift0, s0, wl, *, tb, chunk, t_valid, unroll=1, seqs=1, out_dtype=BF16):
    b, t, _ = z3.shape
    vec = lambda n: pl.BlockSpec((1, n), lambda i, j: (0, 0))
    mat = lambda r, c_: pl.BlockSpec((r, c_), lambda i, j: (0, 0))
    kern = functools.partial(_rwkv_kernel, tb=tb, chunk=chunk, t_valid=t_valid, unroll=unroll, seqs=seqs)
    nck = tb // chunk
    scr = [pltpu.VMEM((H_A, HEAD_DIM_A, HEAD_DIM_A), F32), pltpu.VMEM((1, RWKV_COLS), F32)]
    scr += [pltpu.VMEM((tb, C_A), F32)] * 8
    scr += [pltpu.VMEM((nck, 8, C_A), F32)]
    scr += [pltpu.VMEM((nck, H_A, HEAD_DIM_A, HEAD_DIM_A), F32)] * 2
    t_blk = tb // seqs
    state_spec = pl.BlockSpec((seqs, H_A, HEAD_DIM_A, HEAD_DIM_A), lambda i, j: (i, 0, 0, 0))
    return pl.pallas_call(
        kern,
        grid=(b // seqs, t // t_blk),
        in_specs=[pl.BlockSpec((seqs, t_blk, RWKV_COLS), lambda i, j: (i, j, 0)),
                  pl.BlockSpec((seqs, 1, RWKV_COLS), lambda i, j: (i, 0, 0)),
                  state_spec,
                  vec(RWKV_COLS), vec(C_A), mat(W_LORA, C_A), vec(C_A), mat(A_LORA, C_A),
                  mat(G_LORA, C_A), vec(C_A), vec(C_A), vec(C_A), vec(C_A), vec(C_A)],
        out_specs=[pl.BlockSpec((seqs, t_blk, C_A), lambda i, j: (i, j, 0)), state_spec],
        out_shape=[jax.ShapeDtypeStruct((b, t, C_A), out_dtype),
                   jax.ShapeDtypeStruct((b, H_A, HEAD_DIM_A, HEAD_DIM_A), F32)],
        scratch_shapes=scr,
        compiler_params=_cparams("parallel", "arbitrary"),
        name="rwkv7",
    )(z3, shift0, s0, wl['rwkv_mu'].reshape(1, -1), wl['rwkv_w0'].reshape(1, -1), wl['rwkv_w_up_bf'],
      wl['rwkv_a0'].reshape(1, -1), wl['rwkv_a_up_bf'], wl['rwkv_g_up_bf'],
      wl['rwkv_k_k'].reshape(1, -1), wl['rwkv_k_a'].reshape(1, -1), wl['rwkv_r_k'].reshape(1, -1),
      wl['rwkv_ln_g'].reshape(1, -1), wl['rwkv_ln_b'].reshape(1, -1))


def _fox_norms(zq, zk, fl, qn_g, kn_g, bf):
    bd = _block_ones(C_B, HEAD_DIM_B)
    inv_n = 1.0 / HEAD_DIM_B
    q = zq * lax.rsqrt(_dot_split_lhs(zq * zq, bd, 2) * inv_n + RMS_EPS) * qn_g
    k = zk * lax.rsqrt(_dot_split_lhs(zk * zk, bd, 2) * inv_n + RMS_EPS) * kn_g
    logf = -_softplus(-(fl + bf))
    return q * (HEAD_DIM_B ** -0.5), k, logf


def _fox_norm_kernel(zq_ref, zk_ref, fl_ref, qn_ref, kn_ref, bf_ref, q_out, k_out, lf_out):
    q, k, logf = _fox_norms(zq_ref[...], zk_ref[...], fl_ref[...], qn_ref[...], kn_ref[...], bf_ref[...])
    q_out[...] = q
    k_out[...] = k
    lf_out[...] = logf


def _fox_norm(z2, qn_t, kn_t, bf_pad):
    m = z2.shape[0]
    fix = lambda i: (0, 0)
    return pl.pallas_call(
        _fox_norm_kernel,
        grid=(1,),
        in_specs=[pl.BlockSpec((m, C_B), lambda i: (0, COL_Q // C_B)),
                  pl.BlockSpec((m, C_B), lambda i: (0, COL_K // C_B)),
                  pl.BlockSpec((m, LANE), lambda i: (0, COL_F // LANE)),
                  pl.BlockSpec((1, C_B), fix), pl.BlockSpec((1, C_B), fix), pl.BlockSpec((1, LANE), fix)],
        out_specs=[pl.BlockSpec((m, C_B), fix), pl.BlockSpec((m, C_B), fix), pl.BlockSpec((m, LANE), fix)],
        out_shape=[jax.ShapeDtypeStruct((m, C_B), F32), jax.ShapeDtypeStruct((m, C_B), F32),
                   jax.ShapeDtypeStruct((m, LANE), F32)],
        compiler_params=_cparams("arbitrary"),
        name="fox_norm",
    )(z2, z2, z2, qn_t, kn_t, bf_pad)


N_AUG = 3
FOX_TILE = 512


def _fox_prep_kernel(zq_ref, zk_ref, zv_ref, fl_ref, qn_ref, kn_ref, bf_ref,
                     kt_out, vt_out, lft_out, qat_out, ka_out, vat_out, cum_scr, *, tp):
    ti = pl.program_id(1)

    @pl.when(ti == 0)
    def _():
        cum_scr[...] = jnp.zeros_like(cum_scr)

    q, k, logf = _fox_norms(zq_ref[0], zk_ref[0], fl_ref[0], qn_ref[...], kn_ref[...], bf_ref[...])
    qt = q.T
    kt = k.T
    vt = zv_ref[0].T
    kt_out[0] = kt
    vt_out[0] = vt
    lft_out[0] = logf.T[:H_B, :]
    lincl = (_iota((tp, tp), 0) >= _iota((tp, tp), 1)).astype(BF16)
    cum = _dot_split_rhs(lincl, logf) + cum_scr[...]
    cum_scr[...] = cum[tp - 1:tp, :]
    cumt = cum.T
    lane = _iota((tp, LANE), 1)
    sub = _iota((8, tp), 0)
    ones_rows = jnp.ones((HEAD_DIM_B, tp), F32)
    zero_rows = jnp.zeros((LANE - HEAD_DIM_B - 8, tp), F32)
    for h in range(H_B):
        hs = slice(h * HEAD_DIM_B, (h + 1) * HEAD_DIM_B)
        c_row = jnp.broadcast_to(cumt[h:h + 1, :], (8, tp))
        aug_q = jnp.where((sub >= N_AUG) & (sub < 2 * N_AUG), 1.0, 0.0)
        for i, p in enumerate(_split_bf16(c_row, N_AUG)):
            aug_q = jnp.where(sub == i, p.astype(F32), aug_q)
        qat_out[0, h, 0] = jnp.concatenate([qt[hs, :], aug_q, zero_rows], axis=0).astype(BF16)
        vat_out[0, h, 0] = jnp.concatenate([vt[hs, :], ones_rows], axis=0).astype(BF16)
        c_col = jnp.broadcast_to(cum[:, h:h + 1], (tp, LANE))
        aug_k = jnp.where((lane >= HEAD_DIM_B) & (lane < HEAD_DIM_B + N_AUG), 1.0, 0.0)
        for i, p in enumerate(_split_bf16(c_col, N_AUG)):
            aug_k = jnp.where(lane == HEAD_DIM_B + N_AUG + i, -p.astype(F32), aug_k)
        k2 = k[:, (h // 2) * LANE:(h // 2 + 1) * LANE]
        if h % 2 == 1:
            k2 = pltpu.roll(k2, HEAD_DIM_B, axis=1)
        ka_out[0, h, 0] = jnp.where(lane < HEAD_DIM_B, k2, aug_k).astype(BF16)


def _fox_prep(z3, qn_t, kn_t, bf_pad, tp):
    b, t, _ = z3.shape
    nt = t // tp
    fix = lambda i, j: (0, 0)
    kern = functools.partial(_fox_prep_kernel, tp=tp)
    tr_spec = pl.BlockSpec((1, H_B, 1, LANE, tp), lambda i, j: (i, 0, j, 0, 0))
    return pl.pallas_call(
        kern,
        grid=(b, nt),
        in_specs=[pl.BlockSpec((1, tp, C_B), lambda i, j: (i, j, COL_Q // C_B)),
                  pl.BlockSpec((1, tp, C_B), lambda i, j: (i, j, COL_K // C_B)),
                  pl.BlockSpec((1, tp, C_B), lambda i, j: (i, j, COL_V // C_B)),
                  pl.BlockSpec((1, tp, LANE), lambda i, j: (i, j, COL_F // LANE)),
                  pl.BlockSpec((1, C_B), fix), pl.BlockSpec((1, C_B), fix), pl.BlockSpec((1, LANE), fix)],
        out_specs=[pl.BlockSpec((1, C_B, tp), lambda i, j: (i, 0, j)),
                   pl.BlockSpec((1, C_B, tp), lambda i, j: (i, 0, j)),
                   pl.BlockSpec((1, H_B, tp), lambda i, j: (i, 0, j)),
                   tr_spec,
                   pl.BlockSpec((1, H_B, 1, tp, LANE), lambda i, j: (i, 0, j, 0, 0)),
                   tr_spec],
        out_shape=[jax.ShapeDtypeStruct((b, C_B, t), F32),
                   jax.ShapeDtypeStruct((b, C_B, t), F32),
                   jax.ShapeDtypeStruct((b, H_B, t), F32),
                   jax.ShapeDtypeStruct((b, H_B, nt, LANE, tp), BF16),
                   jax.ShapeDtypeStruct((b, H_B, nt, tp, LANE), BF16),
                   jax.ShapeDtypeStruct((b, H_B, nt, LANE, tp), BF16)],
        scratch_shapes=[pltpu.VMEM((1, LANE), F32)],
        compiler_params=_cparams("parallel", "arbitrary"),
        name="fox_prep",
    )(z3, z3, z3, z3, qn_t, kn_t, bf_pad)


def _fox_attn_kernel(qt_ref, k_ref, vt_ref, o_ref, m0_scr, m1_scr, acc0_scr, acc1_scr, *, tq):
    qi = pl.program_id(2)
    m_scrs = (m0_scr, m1_scr)
    acc_scrs = (acc0_scr, acc1_scr)
    for hh in range(2):
        m_scrs[hh][...] = jnp.full_like(m_scrs[hh], NEG_BIG)
        acc_scrs[hh][...] = jnp.zeros_like(acc_scrs[hh])
    two = range(2)

    def block(kj, diagonal):
        st = [jnp.dot(k_ref[0, hh, kj], qt_ref[0, hh, 0], preferred_element_type=F32) for hh in two]
        if diagonal:
            keep = _iota((tq, tq), 0) <= _iota((tq, tq), 1)
            st = [jnp.where(keep, s, -jnp.inf) for s in st]
        m_old = [m_scrs[hh][...] for hh in two]
        m_new = [jnp.maximum(m_old[hh], jnp.max(st[hh], axis=0, keepdims=True)) for hh in two]
        alpha = [jnp.exp(m_old[hh] - m_new[hh]) for hh in two]
        p = [jnp.exp(st[hh] - m_new[hh]).astype(BF16) for hh in two]
        pv = [jnp.dot(vt_ref[0, hh, kj], p[hh], preferred_element_type=F32) for hh in two]
        for hh in two:
            acc_scrs[hh][...] = alpha[hh] * acc_scrs[hh][...] + pv[hh]
            m_scrs[hh][...] = m_new[hh]

    def body(kj, carry):
        block(kj, False)
        return carry

    lax.fori_loop(0, qi, body, 0)
    block(qi, True)
    ot = [acc_scrs[hh][0:HEAD_DIM_B, :] / acc_scrs[hh][HEAD_DIM_B:HEAD_DIM_B + 1, :] for hh in two]
    o_ref[0] = jnp.concatenate(ot, axis=0).T.astype(o_ref.dtype)


def _fox_attn(qat, ka, vat):
    b, _, nt, _, tq = qat.shape
    kern = functools.partial(_fox_attn_kernel, tq=tq)
    return pl.pallas_call(
        kern,
        grid=(b, H_B // 2, nt),
        in_specs=[pl.BlockSpec((1, 2, 1, LANE, tq), lambda i, h, j: (i, h, j, 0, 0)),
                  pl.BlockSpec((1, 2, nt, tq, LANE), lambda i, h, j: (i, h, 0, 0, 0)),
                  pl.BlockSpec((1, 2, nt, LANE, tq), lambda i, h, j: (i, h, 0, 0, 0))],
        out_specs=pl.BlockSpec((1, tq, LANE), lambda i, h, j: (i, j, h)),
        out_shape=jax.ShapeDtypeStruct((b, nt * tq, C_B), BF16),
        scratch_shapes=[pltpu.VMEM((1, tq), F32), pltpu.VMEM((1, tq), F32),
                        pltpu.VMEM((LANE, tq), F32), pltpu.VMEM((LANE, tq), F32)],
        compiler_params=_cparams("parallel", "parallel", "arbitrary"),
        name="fox_attn",
    )(qat, ka, vat)


PAGES_PER_STEP = 16
N_ROWS = 32
N_GATE_SPLIT = 3


def _fox_paged_kernel(pt_ref, q_ref, kn_ref, vn_ref, lfn_ref, lf_pool_ref, *rest, n_q, n_steps, n_pages):
    pp = PAGES_PER_STEP
    k_refs, v_refs = rest[0:pp], rest[pp:2 * pp]
    o_ref = rest[2 * pp]
    qbd_scr, m_scr, l_scr, acc_scr, car_scr, cq_scr = rest[2 * pp + 1:]
    b = pl.program_id(0)
    j = pl.program_id(1)
    p_sz = PAGE_SIZE
    r_head = _iota((N_ROWS, 1), 0) % H_B
    r_q = _iota((N_ROWS, 1), 0) // H_B
    ki = _iota((p_sz, p_sz), 0)
    kj = _iota((p_sz, p_sz), 1)
    per_query = lambda a: jnp.concatenate([a] * n_q, axis=0)

    def update(s_pages, vt_pages):
        s = jnp.concatenate(s_pages, axis=1)
        m_old = m_scr[...]
        m_new = jnp.maximum(m_old, jnp.max(s, axis=1, keepdims=True))
        alpha = jnp.exp(m_old - m_new)
        p = jnp.exp(s - m_new)
        l_scr[...] = alpha * l_scr[...] + jnp.sum(p, axis=1, keepdims=True)
        m_scr[...] = m_new
        pv = None
        for i, vt in enumerate(vt_pages):
            d = _dot_nt(p[:, i * p_sz:(i + 1) * p_sz], vt)
            pv = d if pv is None else pv + d
        acc_scr[...] = alpha * acc_scr[...] + pv

    @pl.when(j == 0)
    def _():
        lane_head = _iota((N_ROWS, C_B), 1) // HEAD_DIM_B
        qbd = jnp.where(lane_head == r_head, q_ref[0], 0.0).astype(BF16)
        qbd_scr[...] = qbd
        m_scr[...] = jnp.full_like(m_scr, NEG_BIG)
        l_scr[...] = jnp.zeros_like(l_scr)
        acc_scr[...] = jnp.zeros_like(acc_scr)
        car_scr[...] = jnp.zeros_like(car_scr)
        lft = per_query(lfn_ref[0])
        ct = _dot_split_lhs(lft, (ki <= kj).astype(BF16))
        key = _iota((N_ROWS, p_sz), 1)
        cq = jnp.sum(jnp.where(key == r_q, ct, 0.0), axis=1, keepdims=True)
        cq_scr[...] = cq
        s = _dot(qbd, kn_ref[0]) + cq - ct
        s = jnp.where(key <= r_q, s, -jnp.inf)
        update([s], [vn_ref[0]])

    pieces = []
    for i in range(pp):
        lf_page = lf_pool_ref[pt_ref[b, n_pages - 1 - (j * pp + i)]]
        pieces += [p.astype(F32) for p in _split_bf16(lf_page, N_GATE_SPLIT)]
    suffix_and_total = jnp.concatenate([(ki > kj).astype(BF16), jnp.ones((p_sz, p_sz), BF16)], axis=1)
    res = jnp.dot(jnp.concatenate(pieces, axis=0).astype(BF16), suffix_and_total,
                  preferred_element_type=F32)
    qbd = qbd_scr[...]
    cq = cq_scr[...]
    after = car_scr[...]
    s_pages, vt_pages = [], []
    for i in range(pp):
        r0 = i * N_GATE_SPLIT * H_B
        both = res[r0:r0 + H_B, :]
        for n in range(1, N_GATE_SPLIT):
            both = both + res[r0 + n * H_B:r0 + (n + 1) * H_B, :]
        ex = both[:, :p_sz] + after
        after = after + both[:, p_sz:]
        s_pages.append(_dot(qbd, k_refs[i][...].reshape(C_B, p_sz)) + cq + per_query(ex))
        vt_pages.append(v_refs[i][...].reshape(C_B, p_sz))
    car_scr[...] = after
    update(s_pages, vt_pages)

    @pl.when(j == n_steps - 1)
    def _():
        lane_head = _iota((N_ROWS, C_B), 1) // HEAD_DIM_B
        o = jnp.where(lane_head == r_head, acc_scr[...] / l_scr[...], 0.0)
        o_ref[0] = jnp.sum(o.reshape(n_q, H_B, C_B), axis=1)


def _fox_paged(page_table, q32, kn_pad, vn_pad, lfn_pad, cache_k, cache_v, cache_lf, layer, n_q):
    db, n_pages = page_table.shape
    pp = PAGES_PER_STEP
    n_steps = n_pages // pp
    kern = functools.partial(_fox_paged_kernel, n_q=n_q, n_steps=n_steps, n_pages=n_pages)
    n_phys = cache_lf.shape[1]

    def kv_map(i):
        return lambda b, j, pt: (layer, pt[b, n_pages - 1 - (j * pp + i)], 0, 0, 0)

    per_b = lambda b, j, pt: (b, 0, 0)
    kv_specs = [pl.BlockSpec((None, None, H_B, HEAD_DIM_B, PAGE_SIZE), kv_map(i)) for i in range(pp)]
    grid_spec = pltpu.PrefetchScalarGridSpec(
        num_scalar_prefetch=1,
        grid=(db, n_steps),
        in_specs=[pl.BlockSpec((1, N_ROWS, C_B), per_b),
                  pl.BlockSpec((1, C_B, PAGE_SIZE), per_b),
                  pl.BlockSpec((1, C_B, PAGE_SIZE), per_b),
                  pl.BlockSpec((1, H_B, PAGE_SIZE), per_b),
                  pl.BlockSpec((None, n_phys, H_B, PAGE_SIZE), lambda b, j, pt: (layer, 0, 0, 0))]
        + kv_specs + kv_specs,
        out_specs=pl.BlockSpec((1, n_q, C_B), per_b),
        scratch_shapes=[pltpu.VMEM((N_ROWS, C_B), BF16), pltpu.VMEM((N_ROWS, 1), F32),
                        pltpu.VMEM((N_ROWS, 1), F32), pltpu.VMEM((N_ROWS, C_B), F32),
                        pltpu.VMEM((H_B, PAGE_SIZE), F32), pltpu.VMEM((N_ROWS, 1), F32)],
    )
    return pl.pallas_call(
        kern,
        grid_spec=grid_spec,
        out_shape=jax.ShapeDtypeStruct((db, n_q, C_B), F32),
        compiler_params=_cparams("parallel", "arbitrary"),
        name="fox_paged",
    )(page_table, q32, kn_pad, vn_pad, lfn_pad, cache_lf, *([cache_k] * pp), *([cache_v] * pp))


CONV_PAD = 32
CONV_ROWS = 256


def _conv_kernel(z_ref, buf_ref, w_ref, b_ref, g_ref, beta_ref, y_ref, nb_ref, xp_scr, sh_scr, *, t):
    z = z_ref[0]
    u = z[:, :C_CONV] * _sigmoid(z[:, C_CONV:])
    off = CONV_PAD - (CONV_W - 1)
    xp_scr[off:CONV_PAD, :] = buf_ref[0]
    xp_scr[CONV_PAD:CONV_PAD + t, :] = u
    nb_ref[0] = xp_scr[t + off:t + CONV_PAD, :]
    rt = min(t, CONV_ROWS)
    sub = 8
    for r0 in range(0, t, rt):
        acc = jnp.zeros((rt, C_CONV), F32) + b_ref[...]
        for s in range(sub):
            taps = [j for j in range(CONV_W) if (off + j) % sub == s]
            if not taps:
                continue
            span = sub * max((off + j) // sub for j in taps) + rt
            if s == 0:
                src, base = xp_scr, r0
            else:
                sh_scr[0:span, :] = xp_scr[r0 + s:r0 + s + span, :]
                src, base = sh_scr, 0
            for j in taps:
                q0 = base + sub * ((off + j) // sub)
                acc = acc + w_ref[j:j + 1, :] * src[q0:q0 + rt, :]
        mean = jnp.mean(acc, axis=-1, keepdims=True)
        xc = acc - mean
        var = jnp.mean(xc * xc, axis=-1, keepdims=True)
        yn = xc * lax.rsqrt(var + LN_EPS) * g_ref[...] + beta_ref[...]
        y_ref[0, r0:r0 + rt, :] = (yn * _sigmoid(yn)).astype(y_ref.dtype)


def _conv_module(z3, buf, w, bias, ln_g, ln_b, out_dtype=BF16):
    b, t, _ = z3.shape
    fix = lambda i: (0, 0)
    kern = functools.partial(_conv_kernel, t=t)
    return pl.pallas_call(
        kern,
        grid=(b,),
        in_specs=[pl.BlockSpec((1, t, 2 * C_CONV), lambda i: (i, 0, COL_C // (2 * C_CONV))),
                  pl.BlockSpec((1, CONV_W - 1, C_CONV), lambda i: (i, 0, 0)),
                  pl.BlockSpec((CONV_W, C_CONV), fix), pl.BlockSpec((1, C_CONV), fix),
                  pl.BlockSpec((1, C_CONV), fix), pl.BlockSpec((1, C_CONV), fix)],
        out_specs=[pl.BlockSpec((1, t, C_CONV), lambda i: (i, 0, 0)),
                   pl.BlockSpec((1, CONV_W - 1, C_CONV), lambda i: (i, 0, 0))],
        out_shape=[jax.ShapeDtypeStruct((b, t, C_CONV), out_dtype),
                   jax.ShapeDtypeStruct((b, CONV_W - 1, C_CONV), F32)],
        scratch_shapes=[pltpu.VMEM((t + CONV_PAD, C_CONV), F32),
                        pltpu.VMEM((min(t, CONV_ROWS) + CONV_PAD, C_CONV), F32)],
        compiler_params=_cparams("parallel"),
        name="conv_module",
    )(z3, buf, w, bias.reshape(1, -1), ln_g.reshape(1, -1), ln_b.reshape(1, -1))


def _pre_attn_kernel(x_ref, ya_ref, yb_ref, yc_ref, woa_ref, wob_ref, woc_ref, gx_ref, wq_ref, qn_ref,
                     x1_ref, q_ref):
    x1 = (x_ref[...] + jnp.dot(ya_ref[...], woa_ref[...], preferred_element_type=F32)
          + jnp.dot(yb_ref[...], wob_ref[...], preferred_element_type=F32)
          + jnp.dot(yc_ref[...], woc_ref[...], preferred_element_type=F32))
    x1_ref[...] = x1
    q = jnp.dot(_rms(x1, gx_ref[...]).astype(BF16), wq_ref[...], preferred_element_type=F32)
    for h in range(H_X):
        sl = slice(h * HEAD_DIM_X, (h + 1) * HEAD_DIM_X)
        q_ref[:, sl] = (_rms(q[:, sl], qn_ref[...]) * (HEAD_DIM_X ** -0.5)).astype(q_ref.dtype)


def _pre_attn(x2d, ya, yb, yc, wl, tm):
    m, d = x2d.shape
    row = lambda i: (i, 0)
    fix = lambda i: (0, 0)
    return pl.pallas_call(
        _pre_attn_kernel,
        grid=(m // tm,),
        in_specs=[pl.BlockSpec((tm, d), row), pl.BlockSpec((tm, C_A), row), pl.BlockSpec((tm, C_B), row),
                  pl.BlockSpec((tm, C_CONV), row),
                  pl.BlockSpec((C_A, d), fix), pl.BlockSpec((C_B, d), fix), pl.BlockSpec((C_CONV, d), fix),
                  pl.BlockSpec((1, d), fix), pl.BlockSpec((d, d), fix), pl.BlockSpec((1, HEAD_DIM_X), fix)],
        out_specs=[pl.BlockSpec((tm, d), row), pl.BlockSpec((tm, d), row)],
        out_shape=[jax.ShapeDtypeStruct((m, d), F32), jax.ShapeDtypeStruct((m, d), BF16)],
        compiler_params=_cparams("parallel"),
        name="pre_attn",
    )(x2d, ya, yb, yc, wl['w_out_a'], wl['w_out_b'], wl['w_out_c'], wl['g_x'].reshape(1, d),
      wl['w_xq_bf'], wl['xq_norm'].reshape(1, HEAD_DIM_X))


MEM_LANE_CHUNKS = HEAD_DIM_X // LANE
MEM_ROWS_PER_TOKEN = H_X * MEM_LANE_CHUNKS


def _xattn_kernel(q_ref, k_ref, v_ref, o_ref, *, row_major):
    q = q_ref[0]

    def head_of(ref, h):
        if row_major:
            return ref[0, :, h * HEAD_DIM_X:(h + 1) * HEAD_DIM_X].astype(BF16)
        parts = [ref[0, pl.ds(c * H_X + h, N_MEM, stride=MEM_ROWS_PER_TOKEN), :]
                 for c in range(MEM_LANE_CHUNKS)]
        return jnp.concatenate(parts, axis=1).astype(BF16)

    heads = range(H_X)
    s = [lax.dot_general(q[:, h * HEAD_DIM_X:(h + 1) * HEAD_DIM_X], head_of(k_ref, h),
                         (((1,), (1,)), ((), ())), preferred_element_type=F32) for h in heads]
    m = [jnp.max(s[h], axis=1, keepdims=True) for h in heads]
    p = [jnp.exp(s[h] - m[h]) for h in heads]
    p = [p[h] / jnp.sum(p[h], axis=1, keepdims=True) for h in heads]
    o = [jnp.dot(p[h].astype(BF16), head_of(v_ref, h), preferred_element_type=F32) for h in heads]
    o_ref[0] = jnp.concatenate(o, axis=1).astype(o_ref.dtype)


def _xattn(q3, mk, mv, tq, row_major=True, layer=None):
    b, t, d = q3.shape
    if layer is None:
        kv_spec = pl.BlockSpec((1,) + mk.shape[1:], lambda i, j: (i, 0, 0))
    else:
        kv_spec = pl.BlockSpec((None, 1) + mk.shape[2:], lambda i, j: (layer, i, 0, 0))
    return pl.pallas_call(
        functools.partial(_xattn_kernel, row_major=row_major),
        grid=(b, t // tq),
        in_specs=[pl.BlockSpec((1, tq, d), lambda i, j: (i, j, 0)), kv_spec, kv_spec],
        out_specs=pl.BlockSpec((1, tq, d), lambda i, j: (i, j, 0)),
        out_shape=jax.ShapeDtypeStruct((b, t, d), BF16),
        compiler_params=_cparams("parallel", "parallel"),
        name="xattn",
    )(q3, mk, mv)


def _mem_device_rows(a):
    n_l, b = a.shape[:2]
    a = a.reshape(n_l, b, N_MEM, H_X, MEM_LANE_CHUNKS, LANE)
    return jnp.transpose(a, (0, 1, 2, 4, 3, 5)).reshape(n_l, b, N_MEM * MEM_ROWS_PER_TOKEN, LANE)


FFN_TN = 256
FFN_CHUNKS = D_FF // FFN_TN


def _ffn_kernel(*refs, tm, tiles_per_seq, seg, carried):
    if carried:
        (x1_ref, o_ref, wo_ref, g_ref, wg_ref, wv_ref, cwg_ref, cwv_ref, cbg_ref, cbv_ref, wd_ref,
         y_ref, hg_ref, hv_ref, x2_scr, xn_scr, acc_scr, cg_scr, cv_scr) = refs
    else:
        (x1_ref, o_ref, wo_ref, g_ref, wg_ref, wv_ref, cwg_ref, cwv_ref, cbg_ref, cbv_ref, wd_ref,
         p1g_ref, p1v_ref, p2g_ref, p2v_ref,
         y_ref, hg_ref, hv_ref, x2_scr, xn_scr, acc_scr) = refs
    mi = pl.program_id(0)
    c = pl.program_id(1)

    @pl.when(c == 0)
    def _():
        x2 = x1_ref[...] + jnp.dot(o_ref[...], wo_ref[...], preferred_element_type=F32)
        x2_scr[...] = x2
        xn_scr[...] = _rms(x2, g_ref[...]).astype(BF16)
        acc_scr[...] = jnp.zeros_like(acc_scr)

    xn = xn_scr[...]
    row = _iota((tm, 1), 0)
    halves = []
    for name in ("g", "v"):
        w_ref, cw_ref, cb_ref = (wg_ref, cwg_ref, cbg_ref) if name == "g" else (wv_ref, cwv_ref, cbv_ref)
        h = jnp.dot(xn, w_ref[...], preferred_element_type=F32)
        hm1 = pltpu.roll(h, 1, axis=0)
        hm2 = pltpu.roll(h, 2, axis=0)
        if carried:
            c_scr = cg_scr if name == "g" else cv_scr
            prev = jnp.where(mi % tiles_per_seq == 0, 0.0, c_scr[c])
            hm1 = jnp.where(row == 0, prev[7:8, :], hm1)
            hm2 = jnp.where(row == 0, prev[6:7, :], jnp.where(row == 1, prev[7:8, :], hm2))
            c_scr[c] = h[tm - 8:tm, :]
            out_ref = hg_ref if name == "g" else hv_ref
            out_ref[0] = h[tm - 8:tm, :]
        else:
            p1 = (p1g_ref if name == "g" else p1v_ref)[...]
            p2 = (p2g_ref if name == "g" else p2v_ref)[...]
            hm1 = jnp.where(row % seg == 0, p1, hm1)
            hm2 = jnp.where(row % seg < 2, p2, hm2)
            out_ref = hg_ref if name == "g" else hv_ref
            out_ref[...] = h
        halves.append(cw_ref[0:1, :] * hm2 + cw_ref[1:2, :] * hm1 + cw_ref[2:3, :] * h + cb_ref[...])
    gate, val = halves
    act = gate * _sigmoid(gate) * val
    acc_scr[...] += jnp.dot(act.astype(BF16), wd_ref[...], preferred_element_type=F32)

    @pl.when(c == FFN_CHUNKS - 1)
    def _():
        y_ref[...] = x2_scr[...] + acc_scr[...]


def _ffn(x1, o, wl, *, tm, tiles_per_seq=1, seg=1, prev=None):
    m, d = x1.shape
    carried = prev is None
    tn = FFN_TN
    row = lambda i, c: (i, 0)
    fix = lambda i, c: (0, 0)
    gcol = lambda i, c: (0, c)
    vcol = lambda i, c: (0, c + FFN_CHUNKS)
    in_specs = [pl.BlockSpec((tm, d), row), pl.BlockSpec((tm, d), row), pl.BlockSpec((d, d), fix),
                pl.BlockSpec((1, d), fix),
                pl.BlockSpec((d, tn), gcol), pl.BlockSpec((d, tn), vcol),
                pl.BlockSpec((FFN_CONV_W, tn), gcol), pl.BlockSpec((FFN_CONV_W, tn), vcol),
                pl.BlockSpec((1, tn), gcol), pl.BlockSpec((1, tn), vcol),
                pl.BlockSpec((tn, d), lambda i, c: (c, 0))]
    args = [x1, o, wl['w_xo_bf'], wl['g_ffn'].reshape(1, d), wl['w_up_bf'], wl['w_up_bf'],
            wl['ffn_conv_w'], wl['ffn_conv_w'], wl['ffn_conv_b'].reshape(1, -1),
            wl['ffn_conv_b'].reshape(1, -1), wl['w_down_bf']]
    scratch = [pltpu.VMEM((tm, d), F32), pltpu.VMEM((tm, d), BF16), pltpu.VMEM((tm, d), F32)]
    if carried:
        h_shape = jax.ShapeDtypeStruct((m // tm, 8, D_FF), F32)
        h_spec = pl.BlockSpec((1, 8, tn), lambda i, c: (i, 0, c))
        scratch += [pltpu.VMEM((FFN_CHUNKS, 8, tn), F32)] * 2
    else:
        p1, p2 = prev
        in_specs += [pl.BlockSpec((tm, tn), lambda i, c: (i, c)),
                     pl.BlockSpec((tm, tn), lambda i, c: (i, c + FFN_CHUNKS))] * 2
        args += [p1, p1, p2, p2]
        h_shape = jax.ShapeDtypeStruct((m, D_FF), F32)
        h_spec = pl.BlockSpec((tm, tn), lambda i, c: (i, c))
    kern = functools.partial(_ffn_kernel, tm=tm, tiles_per_seq=tiles_per_seq, seg=seg, carried=carried)
    return pl.pallas_call(
        kern,
        grid=(m // tm, FFN_CHUNKS),
        in_specs=in_specs,
        out_specs=[pl.BlockSpec((tm, d), row), h_spec, h_spec],
        out_shape=[jax.ShapeDtypeStruct((m, d), F32), h_shape, h_shape],
        scratch_shapes=scratch,
        compiler_params=_cparams("arbitrary", "arbitrary"),
        name="ffn",
    )(*args)


def _prep_weights(w, l):
    wl = {n: a[l] for n, a in w.items()}
    w_in = wl['w_in']
    o_b = RWKV_COLS
    o_c = RWKV_COLS + 3 * C_B + H_B
    pad = jnp.zeros((D_MODEL, LANE - H_B), F32)
    w_in2 = jnp.concatenate([w_in[:, :o_b], w_in[:, o_b:o_b + 3 * C_B], w_in[:, o_c:],
                             w_in[:, o_b + 3 * C_B:o_c], pad], axis=1)
    wl['w_in_bf'] = w_in2.astype(BF16)
    for n in ('rwkv_w_up', 'rwkv_a_up', 'rwkv_g_up', 'w_xq', 'w_xk', 'w_xv', 'w_xo', 'w_up', 'w_down'):
        wl[n + '_bf'] = wl[n].astype(BF16)
    w_out = wl['w_out'].astype(BF16)
    wl['w_out_a'] = w_out[:C_A]
    wl['w_out_b'] = w_out[C_A:C_A + C_B]
    wl['w_out_c'] = w_out[C_A + C_B:]
    wl['fox_q_norm_t'] = jnp.tile(wl['fox_q_norm'], H_B).reshape(1, C_B)
    wl['fox_k_norm_t'] = jnp.tile(wl['fox_k_norm'], H_B).reshape(1, C_B)
    wl['fox_b_f_pad'] = jnp.concatenate([wl['fox_b_f'], jnp.zeros((LANE - H_B,), F32)]).reshape(1, LANE)
    return wl


def _pick(n, prefs):
    for p in prefs:
        if n % p == 0:
            return p
    return n


def _prompt_layer(x, mem2d, wl):
    b, t, d = x.shape
    m = b * t
    tm = _pick(m, (512, 256, 128, 64, 32, 16, 8))
    mk, mv = _mem_kv(mem2d, wl['g_mem'], wl['w_xk_bf'], wl['w_xv_bf'], wl['xk_norm'],
                     _pick(mem2d.shape[0], (512, 256)))
    z2 = _norm_matmul(x.reshape(m, d), wl['g_mix'], wl['w_in_bf'], tm)
    z3 = z2.reshape(b, t, IN_COLS_PAD)
    tb = _pick(t, (512, 256, 128, 64))
    ya, s_t = _rwkv(z3, jnp.zeros((b, 1, RWKV_COLS), F32), jnp.zeros((b, H_A, HEAD_DIM_A, HEAD_DIM_A), F32),
                    wl, tb=tb, chunk=min(RWKV_CHUNK, tb), t_valid=tb, unroll=RWKV_UNROLL)
    tp = _pick(t, (FOX_TILE, 256, 128))
    kt, vt, lft, qat, ka, vat = _fox_prep(z3, wl['fox_q_norm_t'], wl['fox_k_norm_t'], wl['fox_b_f_pad'], tp)
    yb = _fox_attn(qat, ka, vat)
    yc, conv_buf = _conv_module(z3, jnp.zeros((b, CONV_W - 1, C_CONV), F32), wl['conv_w'], wl['conv_b'],
                                wl['conv_ln_g'], wl['conv_ln_b'])
    x1, q = _pre_attn(x.reshape(m, d), ya.reshape(m, C_A), yb.reshape(m, C_B), yc.reshape(m, C_CONV), wl, tm)
    o = _xattn(q.reshape(b, t, d), mk.reshape(b, N_MEM, d), mv.reshape(b, N_MEM, d),
               _pick(t, (512, 256, 128, 64)))
    tf = _pick(t, (1024, 512, 256, 128, 64))
    y, hg, hv = _ffn(x1, o.reshape(m, d), wl, tm=tf, tiles_per_seq=t // tf)
    tiles = t // tf
    ffn_buf = jnp.concatenate([hg.reshape(b, tiles, 8, D_FF)[:, -1, 6:], hv.reshape(b, tiles, 8, D_FF)[:, -1, 6:]],
                              axis=-1)
    st = (jnp.transpose(kt.reshape(b, H_B, HEAD_DIM_B, t), (0, 3, 1, 2)),
          jnp.transpose(vt.reshape(b, H_B, HEAD_DIM_B, t), (0, 3, 1, 2)),
          jnp.transpose(lft, (0, 2, 1)),
          s_t,
          z3[:, t - 1:t, :RWKV_COLS],
          conv_buf,
          ffn_buf)
    return y.reshape(b, t, d), st, mk.reshape(b, N_MEM, H_X, HEAD_DIM_X), mv.reshape(b, N_MEM, H_X, HEAD_DIM_X)


def _sample_layer(x, wl, layer, cache_k, cache_v, cache_lf, page_table, s0, shift0, conv0, ffn0, mk, mv):
    b, t, d = x.shape
    m = b * t
    z2 = _norm_matmul(x.reshape(m, d), wl['g_mix'], wl['w_in_bf'], m)
    z3 = z2.reshape(b, t, IN_COLS_PAD)
    t_pad = 8
    za = jnp.pad(z3[:, :, :RWKV_COLS], ((0, 0), (0, t_pad - t), (0, 0)))
    seqs = _pick(b, (8, 4, 2, 1))
    ya, s_t = _rwkv(za, shift0, s0, wl, tb=t_pad * seqs, chunk=t_pad, t_valid=t, unroll=RWKV_UNROLL,
                    seqs=seqs, out_dtype=F32)
    ya = ya[:, :t].astype(BF16)
    q, k_n, lf = _fox_norm(z2, wl['fox_q_norm_t'], wl['fox_k_norm_t'], wl['fox_b_f_pad'])
    zv = z3[:, :, COL_V:COL_V + C_B]
    page_t = lambda a: jnp.pad(jnp.swapaxes(a, 1, 2), ((0, 0), (0, 0), (0, PAGE_SIZE - t)))
    q32 = jnp.repeat(q.reshape(b, t, C_B), H_B, axis=1)
    yb = _fox_paged(page_table, q32, page_t(k_n.reshape(b, t, C_B)), page_t(zv),
                    page_t(lf.reshape(b, t, LANE)[:, :, :H_B]), cache_k, cache_v, cache_lf, layer, t)
    yb = yb.astype(BF16)
    yc, conv_buf = _conv_module(z3, conv0, wl['conv_w'], wl['conv_b'], wl['conv_ln_g'], wl['conv_ln_b'], F32)
    yc = yc.astype(BF16)
    x1, q = _pre_attn(x.reshape(m, d), ya.reshape(m, C_A), yb.reshape(m, C_B), yc.reshape(m, C_CONV), wl, m)
    t16 = 16
    qp = jnp.pad(q.reshape(b, t, d), ((0, 0), (0, t16 - t), (0, 0)))
    o = _xattn(qp, mk, mv, t16, row_major=False, layer=layer)[:, :t]
    zrow = jnp.zeros((b, 1, 2 * D_FF), F32)
    p1 = jnp.concatenate([ffn0[:, 1:2], jnp.tile(zrow, (1, t - 1, 1))], axis=1).reshape(m, 2 * D_FF)
    p2 = jnp.concatenate([ffn0, jnp.tile(zrow, (1, t - 2, 1))], axis=1).reshape(m, 2 * D_FF)
    y, hg, hv = _ffn(x1, o.reshape(m, d), wl, tm=m, seg=t, prev=(p1, p2))
    ffn_buf = jnp.concatenate([hg.reshape(b, t, D_FF)[:, t - 2:], hv.reshape(b, t, D_FF)[:, t - 2:]], axis=-1)
    st = (k_n.reshape(b, t, H_B, HEAD_DIM_B), zv.reshape(b, t, H_B, HEAD_DIM_B),
          lf.reshape(b, t, LANE)[:, :, :H_B], s_t, z3[:, t - 1:t, :RWKV_COLS], conv_buf, ffn_buf)
    return y.reshape(b, t, d), st


def kernel(x_prompt, x_sample, mem_prompt, cache_fox_k, cache_fox_v, cache_fox_logf, page_table, state_rwkv, state_rwkv_shift, state_conv, state_ffn, cache_mem_k, cache_mem_v, g_mix, w_in, rwkv_mu, rwkv_w0, rwkv_w_up, rwkv_a0, rwkv_a_up, rwkv_g_up, rwkv_k_k, rwkv_k_a, rwkv_r_k, rwkv_ln_g, rwkv_ln_b, fox_q_norm, fox_k_norm, fox_b_f, conv_w, conv_b, conv_ln_g, conv_ln_b, w_out, g_x, g_mem, w_xq, w_xk, w_xv, xq_norm, xk_norm, w_xo, g_ffn, w_up, ffn_conv_w, ffn_conv_b, w_down):
    w = dict(g_mix=g_mix, w_in=w_in, rwkv_mu=rwkv_mu, rwkv_w0=rwkv_w0, rwkv_w_up=rwkv_w_up, rwkv_a0=rwkv_a0,
             rwkv_a_up=rwkv_a_up, rwkv_g_up=rwkv_g_up, rwkv_k_k=rwkv_k_k, rwkv_k_a=rwkv_k_a, rwkv_r_k=rwkv_r_k,
             rwkv_ln_g=rwkv_ln_g, rwkv_ln_b=rwkv_ln_b, fox_q_norm=fox_q_norm, fox_k_norm=fox_k_norm,
             fox_b_f=fox_b_f, conv_w=conv_w, conv_b=conv_b, conv_ln_g=conv_ln_g, conv_ln_b=conv_ln_b,
             w_out=w_out, g_x=g_x, g_mem=g_mem, w_xq=w_xq, w_xk=w_xk, w_xv=w_xv, xq_norm=xq_norm,
             xk_norm=xk_norm, w_xo=w_xo, g_ffn=g_ffn, w_up=w_up, ffn_conv_w=ffn_conv_w,
             ffn_conv_b=ffn_conv_b, w_down=w_down)
    depth = w_in.shape[0]
    wls = [_prep_weights(w, l) for l in range(depth)]

    b, n_mem, d = mem_prompt.shape
    mem2d = mem_prompt.reshape(b * n_mem, d)
    x = x_prompt
    p_st, p_mk, p_mv = [], [], []
    for l in range(depth):
        x, st, mk, mv = _prompt_layer(x, mem2d, wls[l])
        p_st.append(st)
        p_mk.append(mk)
        p_mv.append(mv)
    y_prompt = x
    p_out = tuple(jnp.stack(f) for f in zip(*p_st))

    ck = jnp.transpose(cache_fox_k, (0, 1, 3, 4, 2))
    cv = jnp.transpose(cache_fox_v, (0, 1, 3, 4, 2))
    clf = jnp.transpose(cache_fox_logf, (0, 1, 3, 2))
    mem_k_rows = _mem_device_rows(cache_mem_k)
    mem_v_rows = _mem_device_rows(cache_mem_v)
    x = x_sample
    s_st = []
    for l in range(depth):
        x, st = _sample_layer(x, wls[l], l, ck, cv, clf, page_table, state_rwkv[l],
                              state_rwkv_shift[l], state_conv[l], state_ffn[l], mem_k_rows, mem_v_rows)
        s_st.append(st)
    y_sample = x
    s_out = tuple(jnp.stack(f) for f in zip(*s_st))

    return (y_prompt, y_sample) + p_out + (jnp.stack(p_mk), jnp.stack(p_mv)) + s_out
```
